```python
import math
import jax, jax.numpy as jnp
from jax import lax
import numpy as np

D_MODEL = 4096
BATCH = 1
SEQ = 8192
DEPTH = 4

N_A_LAYERS = DEPTH // 2
N_B_LAYERS = DEPTH - N_A_LAYERS
MIX_HEAD_DIM = 128
MIX_WIDTH = 3 * D_MODEL // 4
MIX_HEADS = MIX_WIDTH // MIX_HEAD_DIM
MEM_HEADS = 4
MEM_WIDTH = D_MODEL - MIX_WIDTH
MEM_HEAD_DIM = MEM_WIDTH // MEM_HEADS
N_MEM = 256
HGRN_CHUNK = 64
SB_BLOCK = 128
D_FF = 256 * ((8 * D_MODEL // 3 + 255) // 256)
CONV_WIDTH = 3
LN_EPS = 1e-5
RMS_EPS = 1e-6
LB_TINY = 1e-30
DEEPNORM_ALPHA = (2 * DEPTH) ** 0.25
DEEPNORM_BETA = (8 * DEPTH) ** -0.25
A_IN_WIDTH = 4 * MIX_WIDTH + MEM_WIDTH
B_IN_WIDTH = MIX_WIDTH + MEM_WIDTH

kernel_name = "yoco_hgrn2_stickbreaking_hybrid"


def layer_norm(x, g, b):
    xf = x.astype(jnp.float32)
    mu = xf.mean(-1, keepdims=True)
    var = jnp.square(xf - mu).mean(-1, keepdims=True)
    return ((xf - mu) * lax.rsqrt(var + LN_EPS) * g + b).astype(x.dtype)


def split_heads(t, head_dim):
    return t.reshape(t.shape[0], t.shape[1], -1, head_dim)


def hgrn2_chunkwise(q, k, v, log_f):
    B, S, H, DK = q.shape
    DV = v.shape[-1]
    nc = S // HGRN_CHUNK

    def to_chunks(t):
        return t.astype(jnp.float32).reshape(B, nc, HGRN_CHUNK, H, t.shape[-1]).transpose(1, 0, 3, 2, 4)

    qc, kc, vc, gc = to_chunks(q), to_chunks(k), to_chunks(v), to_chunks(log_f)
    causal = jnp.tril(jnp.ones((HGRN_CHUNK, HGRN_CHUNK), dtype=bool))[:, :, None]

    def step(state, inp):
        qb, kb, vb, gb = inp
        b = jnp.cumsum(gb, axis=-2)
        diff = b[..., :, None, :] - b[..., None, :, :]
        decay = jnp.where(causal, jnp.exp(jnp.where(causal, diff, 0.0)), 0.0)
        scores = jnp.einsum('bhtd,bhtsd,bhsd->bhts', qb, decay, kb)
        o = (jnp.einsum('bhts,bhsv->bhtv', scores, vb)
             + jnp.einsum('bhtd,bhdv->bhtv', qb * jnp.exp(b), state))
        b_last = b[..., -1:, :]
        new_state = (jnp.exp(b_last)[..., 0, :, None] * state
                     + jnp.einsum('bhsd,bhsv->bhdv', kb * jnp.exp(b_last - b), vb))
        return new_state, o

    state0 = jnp.zeros((B, H, DK, DV), jnp.float32)
    _, out = lax.scan(step, state0, (qc, kc, vc, gc))
    return out.transpose(1, 0, 3, 2, 4).reshape(B, S, H, DV).astype(v.dtype)


def stick_breaking_attention(q, k, v):
    B, S, H, D = q.shape
    nb = S // SB_BLOCK
    scale = D ** -0.5
    kh = k.transpose(0, 2, 1, 3)
    vh = v.transpose(0, 2, 1, 3)
    qb = q.reshape(B, nb, SB_BLOCK, H, D).transpose(1, 0, 3, 2, 4)
    key_pos = jnp.arange(S)

    def one_block(args):
        blk, qblk = args
        q_pos = blk * SB_BLOCK + jnp.arange(SB_BLOCK)
        mask = key_pos[None, :] < q_pos[:, None]
        z = jnp.einsum('bhqd,bhsd->bhqs', qblk, kh).astype(jnp.float32) * scale
        log_beta = jax.nn.log_sigmoid(z)
        log_rest = jnp.where(mask, jax.nn.log_sigmoid(-z), 0.0)
        after = lax.cumsum(log_rest, axis=3, reverse=True) - log_rest
        w = jnp.where(mask, jnp.exp(jnp.where(mask, log_beta + after, 0.0)), 0.0)
        return jnp.einsum('bhqs,bhsd->bhqd', w.astype(v.dtype), vh)

    out = lax.map(one_block, (jnp.arange(nb), qb))
    return out.transpose(1, 0, 3, 2, 4).reshape(B, S, H, D)


def memory_cross_attention(mq, mem, w_mem_kv):
    B, S, _ = mq.shape
    mk, mv = jnp.split(mem @ w_mem_kv, 2, axis=-1)
    qh, kh, vh = split_heads(mq, MEM_HEAD_DIM), split_heads(mk, MEM_HEAD_DIM), split_heads(mv, MEM_HEAD_DIM)
    s = jnp.einsum('bqhd,bmhd->bhqm', qh, kh).astype(jnp.float32) * (MEM_HEAD_DIM ** -0.5)
    p = jax.nn.softmax(s, axis=-1).astype(mq.dtype)
    return jnp.einsum('bhqm,bmhd->bqhd', p, vh).reshape(B, S, MEM_WIDTH)


def hgrn2_mixer(x, mem, w_in, lb, onorm_g, w_mem_kv, w_o):
    B, S, _ = x.shape
    q, fz, i, g, mq = jnp.split(x @ w_in, [MIX_WIDTH, 2 * MIX_WIDTH, 3 * MIX_WIDTH, 4 * MIX_WIDTH], axis=-1)
    q = jax.nn.silu(q)
    fz = fz.astype(jnp.float32)
    log_f = jnp.logaddexp(jnp.log(jnp.maximum(lb, LB_TINY)), jnp.log1p(-lb) + jax.nn.log_sigmoid(fz))
    k = (1.0 - lb) * jax.nn.sigmoid(-fz)
    o = hgrn2_chunkwise(split_heads(q, MIX_HEAD_DIM), split_heads(k, MIX_HEAD_DIM),
                        split_heads(i, MIX_HEAD_DIM), split_heads(log_f, MIX_HEAD_DIM))
    of = o.astype(jnp.float32)
    of = of * lax.rsqrt(jnp.mean(jnp.square(of), axis=-1, keepdims=True) + RMS_EPS)
    o = (of.reshape(B, S, MIX_WIDTH) * onorm_g).astype(x.dtype) * jax.nn.silu(g)
    m = memory_cross_attention(mq, mem, w_mem_kv)
    return jnp.concatenate([o, m], axis=-1) @ w_o


def stick_breaking_mixer(x, mem, k_shared, v_shared, w_in, w_mem_kv, w_o):
    q, mq = jnp.split(x @ w_in, [MIX_WIDTH], axis=-1)
    B, S, _ = x.shape
    o = stick_breaking_attention(split_heads(q, MIX_HEAD_DIM), k_shared, v_shared).reshape(B, S, MIX_WIDTH)
    m = memory_cross_attention(mq, mem, w_mem_kv)
    return jnp.concatenate([o, m], axis=-1) @ w_o


def conv_glu_ffn(x, w_in, conv_w, conv_b, w_out):
    S = x.shape[1]
    gate, up = jnp.split(x @ w_in, 2, axis=-1)
    gp = jnp.pad(gate, ((0, 0), (CONV_WIDTH - 1, 0), (0, 0)))
    conv = conv_b + conv_w[0] * gp[:, 0:S]
    for tap in range(1, CONV_WIDTH):
        conv = conv + conv_w[tap] * gp[:, tap:tap + S]
    return (jax.nn.silu(conv) * up) @ w_out


def setup_inputs(seed: int = 0) -> dict:
    key = jax.random.key(seed)
    ks = jax.random.split(key, 16)

    def nrm(k, shape, scale):
        return jax.random.normal(k, shape, jnp.float32) * scale

    return {
        "x": nrm(ks[0], (BATCH, SEQ, D_MODEL), 1.0),
        "mem": nrm(ks[1], (BATCH, N_MEM, D_MODEL), 1.0),
        "a_w_in": nrm(ks[2], (N_A_LAYERS, D_MODEL, A_IN_WIDTH), D_MODEL ** -0.5),
        "hgrn_lb_logits": nrm(ks[3], (N_A_LAYERS, MIX_WIDTH), 0.5),
        "a_onorm_g": 1.0 + nrm(ks[4], (N_A_LAYERS, MIX_WIDTH), 0.02),
        "b_w_in": nrm(ks[5], (N_B_LAYERS, D_MODEL, B_IN_WIDTH), D_MODEL ** -0.5),
        "w_kv_shared": nrm(ks[6], (D_MODEL, 2 * MIX_WIDTH), D_MODEL ** -0.5),
        "w_mem_kv": nrm(ks[7], (DEPTH, D_MODEL, 2 * MEM_WIDTH), D_MODEL ** -0.5),
        "w_o": nrm(ks[8], (DEPTH, MIX_WIDTH + MEM_WIDTH, D_MODEL), DEEPNORM_BETA * D_MODEL ** -0.5),
        "ffn_w_in": nrm(ks[9], (DEPTH, D_MODEL, 2 * D_FF), D_MODEL ** -0.5),
        "ffn_conv_w": nrm(ks[10], (DEPTH, CONV_WIDTH, D_FF), CONV_WIDTH ** -0.5),
        "ffn_conv_b": nrm(ks[11], (DEPTH, D_FF), 0.02),
        "ffn_w_out": nrm(ks[12], (DEPTH, D_FF, D_MODEL), DEEPNORM_BETA * D_FF ** -0.5),
        "ln_g": 1.0 + nrm(ks[13], (DEPTH, 2, D_MODEL), 0.02),
        "ln_b": nrm(ks[14], (DEPTH, 2, D_MODEL), 0.02),
    }


def reference(x, mem, a_w_in, hgrn_lb_logits, a_onorm_g, b_w_in, w_kv_shared, w_mem_kv,
              w_o, ffn_w_in, ffn_conv_w, ffn_conv_b, ffn_w_out, ln_g, ln_b):
    B, S, _ = x.shape
    sm = jax.nn.softmax(hgrn_lb_logits.astype(jnp.float32), axis=0)
    lower_bounds = jnp.cumsum(sm, axis=0) - sm[0]
    k_shared = v_shared = None
    for layer in range(DEPTH):
        if layer < N_A_LAYERS:
            mix = hgrn2_mixer(x, mem, a_w_in[layer], lower_bounds[layer], a_onorm_g[layer],
                              w_mem_kv[layer], w_o[layer])
        else:
            if layer == N_A_LAYERS:
                kv = x @ w_kv_shared
                k_shared = split_heads(kv[..., :MIX_WIDTH], MIX_HEAD_DIM)
                v_shared = split_heads(kv[..., MIX_WIDTH:], MIX_HEAD_DIM)
            mix = stick_breaking_mixer(x, mem, k_shared, v_shared, b_w_in[layer - N_A_LAYERS],
                                       w_mem_kv[layer], w_o[layer])
        x = layer_norm(DEEPNORM_ALPHA * x + mix, ln_g[layer, 0], ln_b[layer, 0])
        ffn = conv_glu_ffn(x, ffn_w_in[layer], ffn_conv_w[layer], ffn_conv_b[layer], ffn_w_out[layer])
        x = layer_norm(DEEPNORM_ALPHA * x + ffn, ln_g[layer, 1], ln_b[layer, 1])
    return x
```

```python
import functools

import jax
import jax.numpy as jnp
from jax import lax
from jax.experimental import pallas as pl
from jax.experimental.pallas import tpu as pltpu

DEPTH = 4
N_A_LAYERS = DEPTH // 2
MIX_HEAD_DIM = 128
MEM_HEADS = 4
CONV_WIDTH = 3
LN_EPS = 1e-5
RMS_EPS = 1e-6
LB_TINY = 1e-30
DEEPNORM_ALPHA = (2 * DEPTH) ** 0.25

VMEM_LIMIT_BYTES = 56 * 1024 * 1024
SUBLANES = 8
HGRN_SUB = 16

F32 = jnp.float32
BF16 = jnp.bfloat16


def _params(*semantics):
    return pltpu.CompilerParams(dimension_semantics=semantics, vmem_limit_bytes=VMEM_LIMIT_BYTES)


def _dot(a, b):
    return jnp.dot(a, b, preferred_element_type=F32)


def _dot_nt(a, b):
    return lax.dot_general(a, b, (((1,), (1,)), ((), ())), preferred_element_type=F32)


def _dot_tn(a, b):
    return lax.dot_general(a, b, (((0,), (0,)), ((), ())), preferred_element_type=F32)


def _log_sigmoid_parts(z):
    sp = jnp.log1p(jnp.exp(-jnp.abs(z)))
    return jnp.minimum(z, 0.0) - sp, -jnp.maximum(z, 0.0) - sp


def _split_bf16(x):
    hi = x.astype(BF16)
    lo = (x - hi.astype(F32)).astype(BF16)
    return hi, lo


def _mm_kernel(x_ref, w_ref, o_ref):
    o_ref[...] = _dot(x_ref[...], w_ref[...]).astype(o_ref.dtype)


def matmul(x, w, out_dtype, tm, tn):
    m, k = x.shape
    n = w.shape[1]
    tm, tn = min(tm, m), min(tn, n)
    assert m % tm == 0 and n % tn == 0, (m, n, tm, tn)
    return pl.pallas_call(
        _mm_kernel,
        grid=(m // tm, n // tn),
        in_specs=[pl.BlockSpec((tm, k), lambda i, j: (i, 0)),
                  pl.BlockSpec((k, tn), lambda i, j: (0, j))],
        out_specs=pl.BlockSpec((tm, tn), lambda i, j: (i, j)),
        out_shape=jax.ShapeDtypeStruct((m, n), out_dtype),
        compiler_params=_params("parallel", "parallel"),
        name="matmul",
    )(x, w)


def _ln_kernel(x_ref, mix_ref, g_ref, b_ref, xo_ref, xb_ref):
    h = DEEPNORM_ALPHA * x_ref[...] + mix_ref[...]
    mu = jnp.mean(h, axis=-1, keepdims=True)
    d = h - mu
    var = jnp.mean(d * d, axis=-1, keepdims=True)
    y = d * lax.rsqrt(var + LN_EPS) * g_ref[...] + b_ref[...]
    xo_ref[...] = y
    xb_ref[...] = y.astype(BF16)


def residual_layer_norm(x, mix, g, b, tm=256):
    s, d = x.shape
    tm = min(tm, s)
    row = pl.BlockSpec((tm, d), lambda i: (i, 0))
    vec = pl.BlockSpec((1, d), lambda i: (0, 0))
    return pl.pallas_call(
        _ln_kernel,
        grid=(s // tm,),
        in_specs=[row, row, vec, vec],
        out_specs=[row, row],
        out_shape=[jax.ShapeDtypeStruct((s, d), F32), jax.ShapeDtypeStruct((s, d), BF16)],
        compiler_params=_params("parallel"),
        name="residual_layer_norm",
    )(x, mix, g.reshape(1, d), b.reshape(1, d))


def _hgrn_kernel(lbl_ref, q_ref, f_ref, i_ref, g_ref, gn_ref, o_ref,
                 st_ref, qs_s, k_s, b_s, o_s, *, layer, ts):
    dk = MIX_HEAD_DIM
    c = HGRN_SUB

    @pl.when(pl.program_id(1) == 0)
    def _():
        st_ref[...] = jnp.zeros_like(st_ref)

    lg = lbl_ref[...]
    e = jnp.exp(lg - jnp.max(lg, axis=0, keepdims=True))
    sm = e / jnp.sum(e, axis=0, keepdims=True)
    cum = sm[0:1]
    for l in range(1, layer + 1):
        cum = cum + sm[l:l + 1]
    lb = cum - sm[0:1]

    fz = f_ref[...]
    ls_pos, ls_neg = _log_sigmoid_parts(fz)
    la = jnp.log(jnp.maximum(lb, LB_TINY))
    lc = jnp.log1p(-lb) + ls_pos
    log_f = jnp.maximum(la, lc) + jnp.log1p(jnp.exp(-jnp.abs(la - lc)))
    q = q_ref[...]
    qs_s[...] = q * jax.nn.sigmoid(q)
    k_s[...] = (1.0 - lb) * jnp.exp(ls_neg)

    r = lax.broadcasted_iota(jnp.int32, (ts, ts), 0)
    cc = lax.broadcasted_iota(jnp.int32, (ts, ts), 1)
    tri = jnp.where((r // c == cc // c) & (cc <= r), 1.0, 0.0).astype(BF16)
    hi, lo = _split_bf16(log_f)
    b_s[...] = _dot(tri, hi) + _dot(tri, lo)

    ones = jnp.ones((dk, dk), BF16)
    t_idx = lax.broadcasted_iota(jnp.int32, (c, dk), 0)

    def sub_block(j, st):
        rows = pl.ds(pl.multiple_of(j * c, c), c)
        qs, kk, b, vv = qs_s[rows, :], k_s[rows, :], b_s[rows, :], i_ref[rows, :]
        bl = b[c - 1:c, :]
        o_inter = _dot_nt((qs * jnp.exp(b)).astype(BF16), st.astype(BF16))
        kh = kk * jnp.exp(bl - b)
        prods = []
        for s in range(c):
            p = qs * kk[s:s + 1, :] * jnp.exp(b - b[s:s + 1, :])
            prods.append(jnp.where(t_idx >= s, p, 0.0).astype(BF16))
        scores = _dot(jnp.concatenate(prods, axis=0), ones)
        o_diag = scores[0:c] * vv[0:1, :]
        for s in range(1, c):
            o_diag = o_diag + scores[s * c:(s + 1) * c] * vv[s:s + 1, :]
        o_s[rows, :] = o_inter + o_diag
        return st * jnp.exp(bl) + _dot_tn(vv.astype(BF16), kh.astype(BF16))

    st_ref[...] = lax.fori_loop(0, ts // c, sub_block, st_ref[...])

    of = o_s[...]
    of = of * lax.rsqrt(jnp.mean(of * of, axis=-1, keepdims=True) + RMS_EPS)
    g = g_ref[...]
    o_ref[...] = (of * gn_ref[...] * (g * jax.nn.sigmoid(g))).astype(o_ref.dtype)


def hgrn2_mix(y, lb_logits, onorm_g, layer, n_heads, ts=512):
    s = y.shape[0]
    dk = MIX_HEAD_DIM
    ts = min(ts, s)
    n_layers = lb_logits.shape[0]

    def col(group):
        return pl.BlockSpec((ts, dk), lambda h, i, group=group: (i, group * n_heads + h))

    return pl.pallas_call(
        functools.partial(_hgrn_kernel, layer=layer, ts=ts),
        grid=(n_heads, s // ts),
        in_specs=[pl.BlockSpec((n_layers, dk), lambda h, i: (0, h)),
                  col(0), col(1), col(2), col(3),
                  pl.BlockSpec((1, dk), lambda h, i: (0, h))],
        out_specs=pl.BlockSpec((ts, dk), lambda h, i: (i, h)),
        out_shape=jax.ShapeDtypeStruct((s, n_heads * dk), BF16),
        scratch_shapes=[pltpu.VMEM((dk, dk), F32)] + [pltpu.VMEM((ts, dk), F32)] * 4,
        compiler_params=_params("parallel", "arbitrary"),
        name="hgrn2_mix",
    )(lb_logits, y, y, y, y, onorm_g.reshape(1, -1))


def _sb_kernel(q_ref, k_ref, v_ref, o_ref, *, tq, tk):
    d = MIX_HEAD_DIM
    scale = d ** -0.5
    qi = pl.program_id(1)
    q = q_ref[...]
    row_pos = qi * tq + lax.broadcasted_iota(jnp.int32, (tq, tk), 0)
    col_in = lax.broadcasted_iota(jnp.int32, (tq, tk), 1)
    jj = lax.broadcasted_iota(jnp.int32, (tk, tk + d), 0)
    ss = lax.broadcasted_iota(jnp.int32, (tk, tk + d), 1)
    later = jnp.where((jj > ss) | (ss >= tk), 1.0, 0.0).astype(BF16)
    n_blocks = (qi + 1) * (tq // tk)

    def key_block(i, carry):
        acc, rest = carry
        kb = n_blocks - 1 - i
        rows = pl.ds(pl.multiple_of(kb * tk, tk), tk)
        z = _dot_nt(q, k_ref[rows, :]) * scale
        log_beta, log_rest = _log_sigmoid_parts(z)
        mask = (kb * tk + col_in) < row_pos
        log_rest = jnp.where(mask, log_rest, 0.0)
        hi, lo = _split_bf16(log_rest)
        sums = _dot(hi, later) + _dot(lo, later)
        w = jnp.where(mask, jnp.exp(log_beta + sums[:, :tk] + rest), 0.0)
        acc = acc + _dot(w.astype(BF16), v_ref[rows, :])
        return acc, rest + sums[:, tk:]

    zeros = jnp.zeros((tq, d), F32)
    acc, _ = lax.fori_loop(0, n_blocks, key_block, (zeros, zeros))
    o_ref[...] = acc.astype(o_ref.dtype)


def stick_breaking_mix(qy, kv, n_heads, tq=256, tk=128):
    s = qy.shape[0]
    d = MIX_HEAD_DIM
    tq = min(tq, s)
    assert tk == d and tq % tk == 0
    return pl.pallas_call(
        functools.partial(_sb_kernel, tq=tq, tk=tk),
        grid=(n_heads, s // tq),
        in_specs=[pl.BlockSpec((tq, d), lambda h, i: (i, h)),
                  pl.BlockSpec((s, d), lambda h, i: (0, h)),
                  pl.BlockSpec((s, d), lambda h, i: (0, n_heads + h))],
        out_specs=pl.BlockSpec((tq, d), lambda h, i: (i, h)),
        out_shape=jax.ShapeDtypeStruct((s, n_heads * d), BF16),
        compiler_params=_params("parallel", "parallel"),
        name="stick_breaking_mix",
    )(qy, kv, kv)


def _mem_kernel(q_ref, k_ref, v_ref, o_ref):
    hd = q_ref.shape[-1]
    s = _dot_nt(q_ref[...].astype(BF16), k_ref[...]) * (hd ** -0.5)
    e = jnp.exp(s - jnp.max(s, axis=-1, keepdims=True))
    p = e / jnp.sum(e, axis=-1, keepdims=True)
    o_ref[...] = _dot(p.astype(BF16), v_ref[...]).astype(o_ref.dtype)


def memory_mix(qy, q_offset, mem_kv, tq=512):
    s = qy.shape[0]
    n_mem, w2 = mem_kv.shape
    hd = w2 // 2 // MEM_HEADS
    tq = min(tq, s)
    q0 = q_offset // hd
    return pl.pallas_call(
        _mem_kernel,
        grid=(s // tq, MEM_HEADS),
        in_specs=[pl.BlockSpec((tq, hd), lambda i, h: (i, q0 + h)),
                  pl.BlockSpec((n_mem, hd), lambda i, h: (0, h)),
                  pl.BlockSpec((n_mem, hd), lambda i, h: (0, MEM_HEADS + h))],
        out_specs=pl.BlockSpec((tq, hd), lambda i, h: (i, h)),
        out_shape=jax.ShapeDtypeStruct((s, MEM_HEADS * hd), BF16),
        compiler_params=_params("parallel", "parallel"),
        name="memory_mix",
    )(qy, mem_kv, mem_kv)


def _ffn_in_kernel(x_ref, wg_ref, wu_ref, cw_ref, cb_ref, o_ref, tail_ref):
    tm = x_ref.shape[0]

    @pl.when(pl.program_id(1) == 0)
    def _():
        tail_ref[...] = jnp.zeros_like(tail_ref)

    x = x_ref[...]
    gate = _dot(x, wg_ref[...])
    up = _dot(x, wu_ref[...])
    tail = tail_ref[...]
    row = lax.broadcasted_iota(jnp.int32, gate.shape, 0)
    g1 = jnp.where(row == 0, tail[SUBLANES - 1:SUBLANES, :], pltpu.roll(gate, 1, 0))
    g2 = jnp.where(row == 0, tail[SUBLANES - 2:SUBLANES - 1, :],
                   jnp.where(row == 1, tail[SUBLANES - 1:SUBLANES, :], pltpu.roll(gate, 2, 0)))
    cw = cw_ref[...]
    conv = cb_ref[...] + cw[0:1, :] * g2
    conv = conv + cw[1:2, :] * g1
    conv = conv + cw[2:3, :] * gate
    tail_ref[...] = gate[tm - SUBLANES:tm, :]
    o_ref[...] = (conv * jax.nn.sigmoid(conv) * up).astype(o_ref.dtype)


def conv_glu_in(x, w_in, conv_w, conv_b, tm=1024, tn=256):
    s, d = x.shape
    d_ff = w_in.shape[1] // 2
    tm = min(tm, s)
    tn = min(tn, d_ff)
    assert s % tm == 0 and d_ff % tn == 0 and CONV_WIDTH == 3
    nt = d_ff // tn
    return pl.pallas_call(
        _ffn_in_kernel,
        grid=(nt, s // tm),
        in_specs=[pl.BlockSpec((tm, d), lambda j, i: (i, 0)),
                  pl.BlockSpec((d, tn), lambda j, i: (0, j)),
                  pl.BlockSpec((d, tn), lambda j, i: (0, nt + j)),
                  pl.BlockSpec((CONV_WIDTH, tn), lambda j, i: (0, j)),
                  pl.BlockSpec((1, tn), lambda j, i: (0, j))],
        out_specs=pl.BlockSpec((tm, tn), lambda j, i: (i, j)),
        out_shape=jax.ShapeDtypeStruct((s, d_ff), BF16),
        scratch_shapes=[pltpu.VMEM((SUBLANES, tn), F32)],
        compiler_params=_params("parallel", "arbitrary"),
        name="conv_glu_in",
    )(x, w_in, w_in, conv_w, conv_b.reshape(1, d_ff))


def kernel(x, mem, a_w_in, hgrn_lb_logits, a_onorm_g, b_w_in, w_kv_shared, w_mem_kv, w_o,
           ffn_w_in, ffn_conv_w, ffn_conv_b, ffn_w_out, ln_g, ln_b):
    batch, seq, d_model = x.shape
    assert batch == 1
    mix_width = a_onorm_g.shape[1]
    n_heads = mix_width // MIX_HEAD_DIM
    bf = lambda t: t.astype(BF16)

    xf = x[0]
    xb = bf(xf)
    mem_b = bf(mem[0])
    kv = None
    for layer in range(DEPTH):
        mem_kv = matmul(mem_b, bf(w_mem_kv[layer]), BF16, 256, 1024)
        if layer < N_A_LAYERS:
            y = matmul(xb, bf(a_w_in[layer]), F32, 1024, 1024)
            o = hgrn2_mix(y, hgrn_lb_logits, a_onorm_g[layer], layer, n_heads)
            m = memory_mix(y, 4 * mix_width, mem_kv)
        else:
            if layer == N_A_LAYERS:
                kv = matmul(xb, bf(w_kv_shared), BF16, 1024, 1024)
            y = matmul(xb, bf(b_w_in[layer - N_A_LAYERS]), BF16, 1024, 1024)
            o = stick_breaking_mix(y, kv, n_heads)
            m = memory_mix(y, mix_width, mem_kv)
        mix = matmul(jnp.concatenate([o, m], axis=-1), bf(w_o[layer]), F32, 1024, 1024)
        xf, xb = residual_layer_norm(xf, mix, ln_g[layer, 0], ln_b[layer, 0])
        act = conv_glu_in(xb, bf(ffn_w_in[layer]), ffn_conv_w[layer], ffn_conv_b[layer])
        ffn = matmul(act, bf(ffn_w_out[layer]), F32, 512, 512)
        xf, xb = residual_layer_norm(xf, ffn, ln_g[layer, 1], ln_b[layer, 1])
    return xf[None]
```

```python
import functools

import jax
import jax.numpy as jnp
from jax import lax
from jax.experimental import pallas as pl
from jax.experimental.pallas import tpu as pltpu

DEPTH = 4
N_A_LAYERS = DEPTH // 2
MIX_HEAD_DIM = 128
MEM_HEADS = 4
CONV_WIDTH = 3
LN_EPS = 1e-5
RMS_EPS = 1e-6
LB_TINY = 1e-30
DEEPNORM_ALPHA = (2 * DEPTH) ** 0.25
LOG2_E = 1.4426950408889634

VMEM_LIMIT_BYTES = 56 * 1024 * 1024
SUBLANES = 8
HGRN_SUB = 16

F32 = jnp.float32
BF16 = jnp.bfloat16


def _params(*semantics):
    return pltpu.CompilerParams(dimension_semantics=semantics, vmem_limit_bytes=VMEM_LIMIT_BYTES)


def _dot(a, b):
    return jnp.dot(a, b, preferred_element_type=F32)


def _dot_nt(a, b):
    return lax.dot_general(a, b, (((1,), (1,)), ((), ())), preferred_element_type=F32)


def _dot_tn(a, b):
    return lax.dot_general(a, b, (((0,), (0,)), ((), ())), preferred_element_type=F32)


def _log_sigmoid_parts(z):
    sp = jnp.log1p(jnp.exp(-jnp.abs(z)))
    return jnp.minimum(z, 0.0) - sp, -jnp.maximum(z, 0.0) - sp


def _neg_abs(x):
    bits = lax.bitcast_convert_type(x, jnp.uint32) | jnp.uint32(0x80000000)
    return lax.bitcast_convert_type(bits, F32)


def _split_bf16(x):
    hi = x.astype(BF16)
    lo = (x - hi.astype(F32)).astype(BF16)
    return hi, lo


def _mm_kernel(x_ref, w_ref, o_ref):
    o_ref[...] = _dot(x_ref[...], w_ref[...]).astype(o_ref.dtype)


def matmul(x, w, out_dtype, tm, tn):
    m, k = x.shape
    n = w.shape[1]
    tm, tn = min(tm, m), min(tn, n)
    assert m % tm == 0 and n % tn == 0, (m, n, tm, tn)
    return pl.pallas_call(
        _mm_kernel,
        grid=(m // tm, n // tn),
        in_specs=[pl.BlockSpec((tm, k), lambda i, j: (i, 0)),
                  pl.BlockSpec((k, tn), lambda i, j: (0, j))],
        out_specs=pl.BlockSpec((tm, tn), lambda i, j: (i, j)),
        out_shape=jax.ShapeDtypeStruct((m, n), out_dtype),
        compiler_params=_params("parallel", "parallel"),
        name="matmul",
    )(x, w)


def _ln_kernel(x_ref, mix_ref, g_ref, b_ref, xo_ref, xb_ref):
    h = DEEPNORM_ALPHA * x_ref[...] + mix_ref[...]
    mu = jnp.mean(h, axis=-1, keepdims=True)
    d = h - mu
    var = jnp.mean(d * d, axis=-1, keepdims=True)
    y = d * lax.rsqrt(var + LN_EPS) * g_ref[...] + b_ref[...]
    xo_ref[...] = y
    xb_ref[...] = y.astype(BF16)


def residual_layer_norm(x, mix, g, b, tm=256):
    s, d = x.shape
    tm = min(tm, s)
    row = pl.BlockSpec((tm, d), lambda i: (i, 0))
    vec = pl.BlockSpec((1, d), lambda i: (0, 0))
    return pl.pallas_call(
        _ln_kernel,
        grid=(s // tm,),
        in_specs=[row, row, vec, vec],
        out_specs=[row, row],
        out_shape=[jax.ShapeDtypeStruct((s, d), F32), jax.ShapeDtypeStruct((s, d), BF16)],
        compiler_params=_params("parallel"),
        name="residual_layer_norm",
    )(x, mix, g.reshape(1, d), b.reshape(1, d))


def _hgrn_kernel(lbl_ref, q_ref, f_ref, i_ref, g_ref, gn_ref, o_ref,
                 st_ref, qs_s, k_s, b_s, o_s, *, layer, ts):
    dk = MIX_HEAD_DIM
    c = HGRN_SUB

    @pl.when(pl.program_id(1) == 0)
    def _():
        st_ref[...] = jnp.zeros_like(st_ref)

    lg = lbl_ref[...]
    e = jnp.exp(lg - jnp.max(lg, axis=0, keepdims=True))
    sm = e / jnp.sum(e, axis=0, keepdims=True)
    cum = sm[0:1]
    for l in range(1, layer + 1):
        cum = cum + sm[l:l + 1]
    lb = cum - sm[0:1]

    fz = f_ref[...]
    ls_pos, ls_neg = _log_sigmoid_parts(fz)
    la = jnp.log(jnp.maximum(lb, LB_TINY))
    lc = jnp.log1p(-lb) + ls_pos
    log_f = jnp.maximum(la, lc) + jnp.log1p(jnp.exp(-jnp.abs(la - lc)))
    q = q_ref[...]
    qs_s[...] = q * jax.nn.sigmoid(q)
    k_s[...] = (1.0 - lb) * jnp.exp(ls_neg)

    r = lax.broadcasted_iota(jnp.int32, (ts, ts), 0)
    cc = lax.broadcasted_iota(jnp.int32, (ts, ts), 1)
    tri = jnp.where((r // c == cc // c) & (cc <= r), 1.0, 0.0).astype(BF16)
    hi, lo = _split_bf16(log_f)
    b_s[...] = _dot(tri, hi) + _dot(tri, lo)

    ones = jnp.ones((dk, dk), BF16)
    t_idx = lax.broadcasted_iota(jnp.int32, (c, dk), 0)

    def sub_block(j, st):
        rows = pl.ds(pl.multiple_of(j * c, c), c)
        qs, kk, b, vv = qs_s[rows, :], k_s[rows, :], b_s[rows, :], i_ref[rows, :]
        bl = b[c - 1:c, :]
        o_inter = _dot_nt((qs * jnp.exp(b)).astype(BF16), st.astype(BF16))
        kh = kk * jnp.exp(bl - b)
        prods = []
        for s in range(c):
            p = qs * kk[s:s + 1, :] * jnp.exp(b - b[s:s + 1, :])
            prods.append(jnp.where(t_idx >= s, p, 0.0).astype(BF16))
        scores = _dot(jnp.concatenate(prods, axis=0), ones)
        o_diag = scores[0:c] * vv[0:1, :]
        for s in range(1, c):
            o_diag = o_diag + scores[s * c:(s + 1) * c] * vv[s:s + 1, :]
        o_s[rows, :] = o_inter + o_diag
        return st * jnp.exp(bl) + _dot_tn(vv.astype(BF16), kh.astype(BF16))

    st_ref[...] = lax.fori_loop(0, ts // c, sub_block, st_ref[...])

    of = o_s[...]
    of = of * lax.rsqrt(jnp.mean(of * of, axis=-1, keepdims=True) + RMS_EPS)
    g = g_ref[...]
    o_ref[...] = (of * gn_ref[...] * (g * jax.nn.sigmoid(g))).astype(o_ref.dtype)


def hgrn2_mix(y, lb_logits, onorm_g, layer, n_heads, ts=512):
    s = y.shape[0]
    dk = MIX_HEAD_DIM
    ts = min(ts, s)
    n_layers = lb_logits.shape[0]

    def col(group):
        return pl.BlockSpec((ts, dk), lambda h, i, group=group: (i, group * n_heads + h))

    return pl.pallas_call(
        functools.partial(_hgrn_kernel, layer=layer, ts=ts),
        grid=(n_heads, s // ts),
        in_specs=[pl.BlockSpec((n_layers, dk), lambda h, i: (0, h)),
                  col(0), col(1), col(2), col(3),
                  pl.BlockSpec((1, dk), lambda h, i: (0, h))],
        out_specs=pl.BlockSpec((ts, dk), lambda h, i: (i, h)),
        out_shape=jax.ShapeDtypeStruct((s, n_heads * dk), BF16),
        scratch_shapes=[pltpu.VMEM((dk, dk), F32)] + [pltpu.VMEM((ts, dk), F32)] * 4,
        compiler_params=_params("parallel", "arbitrary"),
        name="hgrn2_mix",
    )(lb_logits, y, y, y, y, onorm_g.reshape(1, -1))


def _sb_kernel(q_ref, k_ref, v_ref, o_ref, acc_ref, rest_ref, *, t, g):
    d = MIX_HEAD_DIM
    scale2 = d ** -0.5 * LOG2_E
    qi = pl.program_id(1)
    jj = lax.broadcasted_iota(jnp.int32, (t, t + d), 0)
    ss = lax.broadcasted_iota(jnp.int32, (t, t + d), 1)
    later = jnp.where((jj > ss) | (ss >= t), 1.0, 0.0).astype(BF16)
    causal = lax.broadcasted_iota(jnp.int32, (t, t), 1) < lax.broadcasted_iota(jnp.int32, (t, t), 0)

    def key_blocks(kb, on_diagonal):
        rows = pl.ds(pl.multiple_of(kb * t, t), t)
        heads = [slice(h * d, (h + 1) * d) for h in range(g)]
        z2 = [_dot_nt(q_ref[:, c], k_ref[rows, c]) * scale2 for c in heads]
        log_beta, log_rest = [], []
        for z in z2:
            lb = jnp.minimum(z, 0.0) - jnp.log2(1.0 + jnp.exp2(_neg_abs(z)))
            lr = lb - z
            log_beta.append(lb)
            log_rest.append(jnp.where(causal, lr, 0.0) if on_diagonal else lr)
        sums = [_dot(lr.astype(BF16), later) for lr in log_rest]
        for c, lb, sm in zip(heads, log_beta, sums):
            if on_diagonal:
                w = jnp.where(causal, jnp.exp2(lb + sm[:, :t]), 0.0)
                acc_ref[:, c] = _dot(w.astype(BF16), v_ref[rows, c])
                rest_ref[:, c] = sm[:, t:]
            else:
                rest = rest_ref[:, c]
                w = jnp.exp2(lb + sm[:, :t] + jnp.concatenate([rest] * (t // d), axis=1))
                acc_ref[:, c] += _dot(w.astype(BF16), v_ref[rows, c])
                rest_ref[:, c] = rest + sm[:, t:]

    key_blocks(qi, True)

    @pl.loop(0, qi)
    def _(i):
        key_blocks(qi - 1 - i, False)

    o_ref[...] = acc_ref[...].astype(o_ref.dtype)


def stick_breaking_mix(qy, kv, n_heads, t=256, g=4):
    s = qy.shape[0]
    d = MIX_HEAD_DIM
    t = min(t, s)
    g = min(g, n_heads)
    assert t % d == 0 and s % t == 0 and n_heads % g == 0
    return pl.pallas_call(
        functools.partial(_sb_kernel, t=t, g=g),
        grid=(n_heads // g, s // t),
        in_specs=[pl.BlockSpec((t, g * d), lambda h, i: (i, h)),
                  pl.BlockSpec((s, g * d), lambda h, i: (0, h)),
                  pl.BlockSpec((s, g * d), lambda h, i: (0, n_heads // g + h))],
        out_specs=pl.BlockSpec((t, g * d), lambda h, i: (i, h)),
        out_shape=jax.ShapeDtypeStruct((s, n_heads * d), BF16),
        scratch_shapes=[pltpu.VMEM((t, g * d), F32)] * 2,
        compiler_params=_params("parallel", "parallel"),
        name="stick_breaking_mix",
    )(qy, kv, kv)


def _mem_kernel(q_ref, k_ref, v_ref, o_ref):
    hd = q_ref.shape[-1]
    s = _dot_nt(q_ref[...].astype(BF16), k_ref[...]) * (hd ** -0.5)
    e = jnp.exp(s - jnp.max(s, axis=-1, keepdims=True))
    p = e / jnp.sum(e, axis=-1, keepdims=True)
    o_ref[...] = _dot(p.astype(BF16), v_ref[...]).astype(o_ref.dtype)


def memory_mix(qy, q_offset, mem_kv, tq=512):
    s = qy.shape[0]
    n_mem, w2 = mem_kv.shape
    hd = w2 // 2 // MEM_HEADS
    tq = min(tq, s)
    q0 = q_offset // hd
    return pl.pallas_call(
        _mem_kernel,
        grid=(s // tq, MEM_HEADS),
        in_specs=[pl.BlockSpec((tq, hd), lambda i, h: (i, q0 + h)),
                  pl.BlockSpec((n_mem, hd), lambda i, h: (0, h)),
                  pl.BlockSpec((n_mem, hd), lambda i, h: (0, MEM_HEADS + h))],
        out_specs=pl.BlockSpec((tq, hd), lambda i, h: (i, h)),
        out_shape=jax.ShapeDtypeStruct((s, MEM_HEADS * hd), BF16),
        compiler_params=_params("parallel", "parallel"),
        name="memory_mix",
    )(qy, mem_kv, mem_kv)


def _ffn_in_kernel(x_ref, wg_ref, wu_ref, cw_ref, cb_ref, o_ref, tail_ref):
    tm = x_ref.shape[0]

    @pl.when(pl.program_id(1) == 0)
    def _():
        tail_ref[...] = jnp.zeros_like(tail_ref)

    x = x_ref[...]
    gate = _dot(x, wg_ref[...])
    up = _dot(x, wu_ref[...])
    tail = tail_ref[...]
    row = lax.broadcasted_iota(jnp.int32, gate.shape, 0)
    g1 = jnp.where(row == 0, tail[SUBLANES - 1:SUBLANES, :], pltpu.roll(gate, 1, 0))
    g2 = jnp.where(row == 0, tail[SUBLANES - 2:SUBLANES - 1, :],
                   jnp.where(row == 1, tail[SUBLANES - 1:SUBLANES, :], pltpu.roll(gate, 2, 0)))
    cw = cw_ref[...]
    conv = cb_ref[...] + cw[0:1, :] * g2
    conv = conv + cw[1:2, :] * g1
    conv = conv + cw[2:3, :] * gate
    tail_ref[...] = gate[tm - SUBLANES:tm, :]
    o_ref[...] = (conv * jax.nn.sigmoid(conv) * up).astype(o_ref.dtype)


def conv_glu_in(x, w_in, conv_w, conv_b, tm=1024, tn=256):
    s, d = x.shape
    d_ff = w_in.shape[1] // 2
    tm = min(tm, s)
    tn = min(tn, d_ff)
    assert s % tm == 0 and d_ff % tn == 0 and CONV_WIDTH == 3
    nt = d_ff // tn
    return pl.pallas_call(
        _ffn_in_kernel,
        grid=(nt, s // tm),
        in_specs=[pl.BlockSpec((tm, d), lambda j, i: (i, 0)),
                  pl.BlockSpec((d, tn), lambda j, i: (0, j)),
                  pl.BlockSpec((d, tn), lambda j, i: (0, nt + j)),
                  pl.BlockSpec((CONV_WIDTH, tn), lambda j, i: (0, j)),
                  pl.BlockSpec((1, tn), lambda j, i: (0, j))],
        out_specs=pl.BlockSpec((tm, tn), lambda j, i: (i, j)),
        out_shape=jax.ShapeDtypeStruct((s, d_ff), BF16),
        scratch_shapes=[pltpu.VMEM((SUBLANES, tn), F32)],
        compiler_params=_params("parallel", "arbitrary"),
        name="conv_glu_in",
    )(x, w_in, w_in, conv_w, conv_b.reshape(1, d_ff))


def kernel(x, mem, a_w_in, hgrn_lb_logits, a_onorm_g, b_w_in, w_kv_shared, w_mem_kv, w_o,
           ffn_w_in, ffn_conv_w, ffn_conv_b, ffn_w_out, ln_g, ln_b):
    batch, seq, d_model = x.shape
    assert batch == 1
    mix_width = a_onorm_g.shape[1]
    n_heads = mix_width // MIX_HEAD_DIM
    bf = lambda t: t.astype(BF16)

    xf = x[0]
    xb = bf(xf)
    mem_b = bf(mem[0])
    kv = None
    for layer in range(DEPTH):
        mem_kv = matmul(mem_b, bf(w_mem_kv[layer]), BF16, 256, 1024)
        if layer < N_A_LAYERS:
            y = matmul(xb, bf(a_w_in[layer]), F32, 1024, 1024)
            o = hgrn2_mix(y, hgrn_lb_logits, a_onorm_g[layer], layer, n_heads)
            m = memory_mix(y, 4 * mix_width, mem_kv)
        else:
            if layer == N_A_LAYERS:
                kv = matmul(xb, bf(w_kv_shared), BF16, 1024, 1024)
            y = matmul(xb, bf(b_w_in[layer - N_A_LAYERS]), BF16, 1024, 1024)
            o = stick_breaking_mix(y, kv, n_heads)
            m = memory_mix(y, mix_width, mem_kv)
        mix = matmul(jnp.concatenate([o, m], axis=-1), bf(w_o[layer]), F32, 1024, 1024)
        xf, xb = residual_layer_norm(xf, mix, ln_g[layer, 0], ln_b[layer, 0])
        act = conv_glu_in(xb, bf(ffn_w_in[layer]), ffn_conv_w[layer], ffn_conv_b[layer])
        ffn = matmul(act, bf(ffn_w_out[layer]), F32, 512, 512)
        xf, xb = residual_layer_norm(xf, ffn, ln_g[layer, 1], ln_b[layer, 1])
    return xf[None]
```

```python
import functools

import jax
import jax.numpy as jnp
from jax import lax
from jax.experimental import pallas as pl
from jax.experimental.pallas import tpu as pltpu

DEPTH = 4
N_A_LAYERS = DEPTH // 2
MIX_HEAD_DIM = 128
MEM_HEADS = 4
CONV_WIDTH = 3
LN_EPS = 1e-5
RMS_EPS = 1e-6
LB_TINY = 1e-30
DEEPNORM_ALPHA = (2 * DEPTH) ** 0.25
LOG2_E = 1.4426950408889634

VMEM_LIMIT_BYTES = 56 * 1024 * 1024
SUBLANES = 8
HGRN_SUB = 16

F32 = jnp.float32
BF16 = jnp.bfloat16


def _params(*semantics):
    return pltpu.CompilerParams(dimension_semantics=semantics, vmem_limit_bytes=VMEM_LIMIT_BYTES)


def _dot(a, b):
    return jnp.dot(a, b, preferred_element_type=F32)


def _dot_nt(a, b):
    return lax.dot_general(a, b, (((1,), (1,)), ((), ())), preferred_element_type=F32)


def _dot_tn(a, b):
    return lax.dot_general(a, b, (((0,), (0,)), ((), ())), preferred_element_type=F32)


def _neg_abs(x):
    bits = lax.bitcast_convert_type(x, jnp.uint32) | jnp.uint32(0x80000000)
    return lax.bitcast_convert_type(bits, F32)


def _log_sigmoid_parts(z):
    sp = jnp.log(1.0 + jnp.exp(_neg_abs(z)))
    return jnp.minimum(z, 0.0) - sp, -jnp.maximum(z, 0.0) - sp


def _split_bf16(x):
    hi = x.astype(BF16)
    lo = (x - hi.astype(F32)).astype(BF16)
    return hi, lo


def _mm_kernel(x_ref, w_ref, o_ref):
    o_ref[...] = _dot(x_ref[...], w_ref[...]).astype(o_ref.dtype)


def matmul(x, w, out_dtype, tm, tn):
    m, k = x.shape
    n = w.shape[1]
    tm, tn = min(tm, m), min(tn, n)
    assert m % tm == 0 and n % tn == 0, (m, n, tm, tn)
    return pl.pallas_call(
        _mm_kernel,
        grid=(m // tm, n // tn),
        in_specs=[pl.BlockSpec((tm, k), lambda i, j: (i, 0)),
                  pl.BlockSpec((k, tn), lambda i, j: (0, j))],
        out_specs=pl.BlockSpec((tm, tn), lambda i, j: (i, j)),
        out_shape=jax.ShapeDtypeStruct((m, n), out_dtype),
        compiler_params=_params("parallel", "parallel"),
        name="matmul",
    )(x, w)


def _mm_f32w_kernel(*refs):
    *x_refs, w_ref, o_ref, wb_ref = refs

    @pl.when(pl.program_id(1) == 0)
    def _():
        wb_ref[...] = w_ref[...].astype(BF16)

    acc, k0 = None, 0
    for x_ref in x_refs:
        k = x_ref.shape[1]
        part = _dot(x_ref[...], wb_ref[k0:k0 + k, :])
        acc = part if acc is None else acc + part
        k0 += k
    o_ref[...] = acc.astype(o_ref.dtype)


def matmul_f32w(xs, w_stack, layer, out_dtype, tm, tn):
    m = xs[0].shape[0]
    _, k, n = w_stack.shape
    assert sum(x.shape[1] for x in xs) == k
    tm, tn = min(tm, m), min(tn, n)
    assert m % tm == 0 and n % tn == 0, (m, n, tm, tn)
    return pl.pallas_call(
        _mm_f32w_kernel,
        grid=(n // tn, m // tm),
        in_specs=([pl.BlockSpec((tm, x.shape[1]), lambda j, i: (i, 0)) for x in xs]
                  + [pl.BlockSpec((None, k, tn), lambda j, i: (layer, 0, j))]),
        out_specs=pl.BlockSpec((tm, tn), lambda j, i: (i, j)),
        out_shape=jax.ShapeDtypeStruct((m, n), out_dtype),
        scratch_shapes=[pltpu.VMEM((k, tn), BF16)],
        compiler_params=_params("parallel", "arbitrary"),
        name="matmul_f32w",
    )(*xs, w_stack)


def _ln_kernel(x_ref, mix_ref, g_ref, b_ref, xo_ref, xb_ref):
    h = DEEPNORM_ALPHA * x_ref[...] + mix_ref[...]
    mu = jnp.mean(h, axis=-1, keepdims=True)
    d = h - mu
    var = jnp.mean(d * d, axis=-1, keepdims=True)
    y = d * lax.rsqrt(var + LN_EPS) * g_ref[...] + b_ref[...]
    xo_ref[...] = y
    xb_ref[...] = y.astype(BF16)


def residual_layer_norm(x, mix, g, b, tm=256):
    s, d = x.shape
    tm = min(tm, s)
    row = pl.BlockSpec((tm, d), lambda i: (i, 0))
    vec = pl.BlockSpec((1, d), lambda i: (0, 0))
    return pl.pallas_call(
        _ln_kernel,
        grid=(s // tm,),
        in_specs=[row, row, vec, vec],
        out_specs=[row, row],
        out_shape=[jax.ShapeDtypeStruct((s, d), F32), jax.ShapeDtypeStruct((s, d), BF16)],
        compiler_params=_params("parallel"),
        name="residual_layer_norm",
    )(x, mix, g.reshape(1, d), b.reshape(1, d))


def _hgrn_kernel(lbl_ref, q_ref, f_ref, i_ref, g_ref, gn_ref, o_ref,
                 st_ref, qs_s, k_s, b_s, o_s, u_s, stb_s, *, layer, ts):
    dk = MIX_HEAD_DIM
    c = HGRN_SUB

    @pl.when(pl.program_id(1) == 0)
    def _():
        st_ref[...] = jnp.zeros_like(st_ref)

    lg = lbl_ref[...]
    e = jnp.exp(lg - jnp.max(lg, axis=0, keepdims=True))
    sm = e / jnp.sum(e, axis=0, keepdims=True)
    cum = sm[0:1]
    for l in range(1, layer + 1):
        cum = cum + sm[l:l + 1]
    lb = cum - sm[0:1]

    fz = f_ref[...]
    ls_pos, ls_neg = _log_sigmoid_parts(fz)
    la = jnp.log(jnp.maximum(lb, LB_TINY))
    lc = jnp.log1p(-lb) + ls_pos
    log_f = jnp.maximum(la, lc) + jnp.log(1.0 + jnp.exp(_neg_abs(la - lc)))
    q = q_ref[...]
    qs_s[...] = q * jax.nn.sigmoid(q)
    k_s[...] = (1.0 - lb) * jnp.exp(ls_neg)

    r = lax.broadcasted_iota(jnp.int32, (ts, ts), 0)
    cc = lax.broadcasted_iota(jnp.int32, (ts, ts), 1)
    tri = jnp.where((r // c == cc // c) & (cc <= r), 1.0, 0.0).astype(BF16)
    hi, lo = _split_bf16(log_f)
    b_s[...] = _dot(tri, hi) + _dot(tri, lo)

    ones = jnp.ones((dk, dk), BF16)
    t_idx = lax.broadcasted_iota(jnp.int32, (c, dk), 0)

    n_sub = ts // c

    def sub_rows(j):
        return pl.ds(pl.multiple_of(j * c, c), c)

    @functools.partial(lax.fori_loop, 0, n_sub, init_val=None, unroll=8)
    def _(j, _):
        rows = sub_rows(j)
        b = b_s[rows, :]
        kh = k_s[rows, :] * jnp.exp(b[c - 1:c, :] - b)
        u_s[j] = _dot_tn(i_ref[rows, :].astype(BF16), kh.astype(BF16))

    def advance(j, st):
        stb_s[j] = st.astype(BF16)
        return st * jnp.exp(b_s[pl.ds(j * c + c - 1, 1), :]) + u_s[j]

    st_ref[...] = lax.fori_loop(0, n_sub, advance, st_ref[...], unroll=4)

    @functools.partial(lax.fori_loop, 0, n_sub, init_val=None, unroll=8)
    def _(j, _):
        rows = sub_rows(j)
        qs, kk, b, vv = qs_s[rows, :], k_s[rows, :], b_s[rows, :], i_ref[rows, :]
        o_inter = _dot_nt((qs * jnp.exp(b)).astype(BF16), stb_s[j])
        prods = []
        for s in range(c):
            p = qs * kk[s:s + 1, :] * jnp.exp(b - b[s:s + 1, :])
            prods.append(jnp.where(t_idx >= s, p, 0.0).astype(BF16))
        scores = _dot(jnp.concatenate(prods, axis=0), ones)
        o_diag = scores[0:c] * vv[0:1, :]
        for s in range(1, c):
            o_diag = o_diag + scores[s * c:(s + 1) * c] * vv[s:s + 1, :]
        o_s[rows, :] = o_inter + o_diag

    of = o_s[...]
    of = of * lax.rsqrt(jnp.mean(of * of, axis=-1, keepdims=True) + RMS_EPS)
    g = g_ref[...]
    o_ref[...] = (of * gn_ref[...] * (g * jax.nn.sigmoid(g))).astype(o_ref.dtype)


def hgrn2_mix(y, lb_logits, onorm_g, layer, n_heads, ts=512):
    s = y.shape[0]
    dk = MIX_HEAD_DIM
    ts = min(ts, s)
    n_layers = lb_logits.shape[0]

    def col(group):
        return pl.BlockSpec((ts, dk), lambda h, i, group=group: (i, group * n_heads + h))

    return pl.pallas_call(
        functools.partial(_hgrn_kernel, layer=layer, ts=ts),
        grid=(n_heads, s // ts),
        in_specs=[pl.BlockSpec((n_layers, dk), lambda h, i: (0, h)),
                  col(0), col(1), col(2), col(3),
                  pl.BlockSpec((1, dk), lambda h, i: (0, h))],
        out_specs=pl.BlockSpec((ts, dk), lambda h, i: (i, h)),
        out_shape=jax.ShapeDtypeStruct((s, n_heads * dk), BF16),
        scratch_shapes=([pltpu.VMEM((dk, dk), F32)] + [pltpu.VMEM((ts, dk), F32)] * 4
                        + [pltpu.VMEM((ts // HGRN_SUB, dk, dk), F32), pltpu.VMEM((ts // HGRN_SUB, dk, dk), BF16)]),
        compiler_params=_params("parallel", "arbitrary"),
        name="hgrn2_mix",
    )(lb_logits, y, y, y, y, onorm_g.reshape(1, -1))


def _sb_kernel(q_ref, k_ref, v_ref, o_ref, acc_ref, rest_ref, *, t, g):
    d = MIX_HEAD_DIM
    scale2 = d ** -0.5 * LOG2_E
    qi = pl.program_id(1)
    jj = lax.broadcasted_iota(jnp.int32, (t, t + d), 0)
    ss = lax.broadcasted_iota(jnp.int32, (t, t + d), 1)
    later = jnp.where((jj > ss) | (ss >= t), 1.0, 0.0).astype(BF16)
    causal = lax.broadcasted_iota(jnp.int32, (t, t), 1) < lax.broadcasted_iota(jnp.int32, (t, t), 0)

    def key_blocks(kb, on_diagonal):
        rows = pl.ds(pl.multiple_of(kb * t, t), t)
        heads = [slice(h * d, (h + 1) * d) for h in range(g)]
        z2 = [_dot_nt(q_ref[:, c], k_ref[rows, c]) * scale2 for c in heads]
        log_beta, log_rest = [], []
        for z in z2:
            lb = jnp.minimum(z, 0.0) - jnp.log2(1.0 + jnp.exp2(_neg_abs(z)))
            lr = lb - z
            log_beta.append(lb)
            log_rest.append(jnp.where(causal, lr, 0.0) if on_diagonal else lr)
        sums = [_dot(lr.astype(BF16), later) for lr in log_rest]
        for c, lb, sm in zip(heads, log_beta, sums):
            if on_diagonal:
                w = jnp.where(causal, jnp.exp2(lb + sm[:, :t]), 0.0)
                acc_ref[:, c] = _dot(w.astype(BF16), v_ref[rows, c])
                rest_ref[:, c] = sm[:, t:]
            else:
                rest = rest_ref[:, c]
                w = jnp.exp2(lb + sm[:, :t] + jnp.concatenate([rest] * (t // d), axis=1))
                acc_ref[:, c] += _dot(w.astype(BF16), v_ref[rows, c])
                rest_ref[:, c] = rest + sm[:, t:]

    key_blocks(qi, True)

    @pl.loop(0, qi)
    def _(i):
        key_blocks(qi - 1 - i, False)

    o_ref[...] = acc_ref[...].astype(o_ref.dtype)


def stick_breaking_mix(qy, kv, n_heads, t=256, g=8):
    s = qy.shape[0]
    d = MIX_HEAD_DIM
    t = min(t, s)
    g = min(g, n_heads)
    assert t % d == 0 and s % t == 0 and n_heads % g == 0
    return pl.pallas_call(
        functools.partial(_sb_kernel, t=t, g=g),
        grid=(n_heads // g, s // t),
        in_specs=[pl.BlockSpec((t, g * d), lambda h, i: (i, h)),
                  pl.BlockSpec((s, g * d), lambda h, i: (0, h), pipeline_mode=pl.Buffered(1)),
                  pl.BlockSpec((s, g * d), lambda h, i: (0, n_heads // g + h), pipeline_mode=pl.Buffered(1))],
        out_specs=pl.BlockSpec((t, g * d), lambda h, i: (i, h)),
        out_shape=jax.ShapeDtypeStruct((s, n_heads * d), BF16),
        scratch_shapes=[pltpu.VMEM((t, g * d), F32)] * 2,
        compiler_params=_params("parallel", "parallel"),
        name="stick_breaking_mix",
    )(qy, kv, kv)


def _mem_kernel(q_ref, k_ref, v_ref, o_ref):
    hd = q_ref.shape[-1]
    s = _dot_nt(q_ref[...].astype(BF16), k_ref[...]) * (hd ** -0.5)
    e = jnp.exp(s - jnp.max(s, axis=-1, keepdims=True))
    p = e / jnp.sum(e, axis=-1, keepdims=True)
    o_ref[...] = _dot(p.astype(BF16), v_ref[...]).astype(o_ref.dtype)


def memory_mix(qy, q_offset, mem_kv, tq=512):
    s = qy.shape[0]
    n_mem, w2 = mem_kv.shape
    hd = w2 // 2 // MEM_HEADS
    tq = min(tq, s)
    q0 = q_offset // hd
    return pl.pallas_call(
        _mem_kernel,
        grid=(s // tq, MEM_HEADS),
        in_specs=[pl.BlockSpec((tq, hd), lambda i, h: (i, q0 + h)),
                  pl.BlockSpec((n_mem, hd), lambda i, h: (0, h)),
                  pl.BlockSpec((n_mem, hd), lambda i, h: (0, MEM_HEADS + h))],
        out_specs=pl.BlockSpec((tq, hd), lambda i, h: (i, h)),
        out_shape=jax.ShapeDtypeStruct((s, MEM_HEADS * hd), BF16),
        compiler_params=_params("parallel", "parallel"),
        name="memory_mix",
    )(qy, mem_kv, mem_kv)


def _ffn_in_kernel(x_ref, wg_ref, wu_ref, cw_ref, cb_ref, o_ref, tail_ref, wgb_ref, wub_ref):
    tm = x_ref.shape[0]

    @pl.when(pl.program_id(1) == 0)
    def _():
        tail_ref[...] = jnp.zeros_like(tail_ref)
        wgb_ref[...] = wg_ref[...].astype(BF16)
        wub_ref[...] = wu_ref[...].astype(BF16)

    x = x_ref[...]
    gate = _dot(x, wgb_ref[...])
    up = _dot(x, wub_ref[...])
    tail = tail_ref[...]
    row = lax.broadcasted_iota(jnp.int32, gate.shape, 0)
    g1 = jnp.where(row == 0, tail[SUBLANES - 1:SUBLANES, :], pltpu.roll(gate, 1, 0))
    g2 = jnp.where(row == 0, tail[SUBLANES - 2:SUBLANES - 1, :],
                   jnp.where(row == 1, tail[SUBLANES - 1:SUBLANES, :], pltpu.roll(gate, 2, 0)))
    cw = cw_ref[...]
    conv = cb_ref[...] + cw[0:1, :] * g2
    conv = conv + cw[1:2, :] * g1
    conv = conv + cw[2:3, :] * gate
    tail_ref[...] = gate[tm - SUBLANES:tm, :]
    o_ref[...] = (conv * jax.nn.sigmoid(conv) * up).astype(o_ref.dtype)


def conv_glu_in(x, w_in, conv_w, conv_b, layer, tm=1024, tn=256):
    s, d = x.shape
    d_ff = w_in.shape[2] // 2
    tm = min(tm, s)
    tn = min(tn, d_ff)
    assert s % tm == 0 and d_ff % tn == 0 and CONV_WIDTH == 3
    nt = d_ff // tn
    return pl.pallas_call(
        _ffn_in_kernel,
        grid=(nt, s // tm),
        in_specs=[pl.BlockSpec((tm, d), lambda j, i: (i, 0)),
                  pl.BlockSpec((None, d, tn), lambda j, i: (layer, 0, j)),
                  pl.BlockSpec((None, d, tn), lambda j, i: (layer, 0, nt + j)),
                  pl.BlockSpec((None, CONV_WIDTH, tn), lambda j, i: (layer, 0, j)),
                  pl.BlockSpec((None, 1, tn), lambda j, i: (layer, 0, j))],
        out_specs=pl.BlockSpec((tm, tn), lambda j, i: (i, j)),
        out_shape=jax.ShapeDtypeStruct((s, d_ff), BF16),
        scratch_shapes=[pltpu.VMEM((SUBLANES, tn), F32), pltpu.VMEM((d, tn), BF16), pltpu.VMEM((d, tn), BF16)],
        compiler_params=_params("parallel", "arbitrary"),
        name="conv_glu_in",
    )(x, w_in, w_in, conv_w, conv_b.reshape(conv_b.shape[0], 1, d_ff))


def kernel(x, mem, a_w_in, hgrn_lb_logits, a_onorm_g, b_w_in, w_kv_shared, w_mem_kv, w_o,
           ffn_w_in, ffn_conv_w, ffn_conv_b, ffn_w_out, ln_g, ln_b):
    batch, seq, d_model = x.shape
    assert batch == 1
    mix_width = a_onorm_g.shape[1]
    n_heads = mix_width // MIX_HEAD_DIM
    bf = lambda t: t.astype(BF16)

    xf = x[0]
    xb = bf(xf)
    mem_b = bf(mem[0])
    kv = None
    for layer in range(DEPTH):
        mem_kv = matmul_f32w([mem_b], w_mem_kv, layer, BF16, 256, 512)
        if layer < N_A_LAYERS:
            y = matmul_f32w([xb], a_w_in, layer, F32, 1024, 512)
            o = hgrn2_mix(y, hgrn_lb_logits, a_onorm_g[layer], layer, n_heads)
            m = memory_mix(y, 4 * mix_width, mem_kv)
        else:
            if layer == N_A_LAYERS:
                kv = matmul_f32w([xb], w_kv_shared[None], 0, BF16, 1024, 512)
            y = matmul_f32w([xb], b_w_in, layer - N_A_LAYERS, BF16, 1024, 512)
            o = stick_breaking_mix(y, kv, n_heads)
            m = memory_mix(y, mix_width, mem_kv)
        mix = matmul_f32w([o, m], w_o, layer, F32, 1024, 512)
        xf, xb = residual_layer_norm(xf, mix, ln_g[layer, 0], ln_b[layer, 0])
        act = conv_glu_in(xb, ffn_w_in, ffn_conv_w, ffn_conv_b, layer)
        ffn = matmul(act, bf(ffn_w_out[layer]), F32, 512, 512)
        xf, xb = residual_layer_norm(xf, ffn, ln_g[layer, 1], ln_b[layer, 1])
    return xf[None]
```

```python
import functools

import jax
import jax.numpy as jnp
from jax import lax
from jax.experimental import pallas as pl
from jax.experimental.pallas import tpu as pltpu

DEPTH = 4
N_A_LAYERS = DEPTH // 2
MIX_HEAD_DIM = 128
MEM_HEADS = 4
CONV_WIDTH = 3
LN_EPS = 1e-5
RMS_EPS = 1e-6
LB_TINY = 1e-30
DEEPNORM_ALPHA = (2 * DEPTH) ** 0.25
LOG2_E = 1.4426950408889634
SB_UNDERFLOW_LOG2 = -160.0

VMEM_LIMIT_BYTES = 56 * 1024 * 1024
SUBLANES = 8
HGRN_SUB = 16

F32 = jnp.float32
BF16 = jnp.bfloat16


def _params(*semantics):
    return pltpu.CompilerParams(dimension_semantics=semantics, vmem_limit_bytes=VMEM_LIMIT_BYTES)


def _dot(a, b):
    return jnp.dot(a, b, preferred_element_type=F32)


def _dot_nt(a, b):
    return lax.dot_general(a, b, (((1,), (1,)), ((), ())), preferred_element_type=F32)


def _dot_tn(a, b):
    return lax.dot_general(a, b, (((0,), (0,)), ((), ())), preferred_element_type=F32)


def _neg_abs(x):
    bits = lax.bitcast_convert_type(x, jnp.uint32) | jnp.uint32(0x80000000)
    return lax.bitcast_convert_type(bits, F32)


def _log_sigmoid_parts(z):
    sp = jnp.log(1.0 + jnp.exp(_neg_abs(z)))
    return jnp.minimum(z, 0.0) - sp, -jnp.maximum(z, 0.0) - sp


def _split_bf16(x):
    hi = x.astype(BF16)
    lo = (x - hi.astype(F32)).astype(BF16)
    return hi, lo


def _mm_kernel(x_ref, w_ref, o_ref):
    o_ref[...] = _dot(x_ref[...], w_ref[...]).astype(o_ref.dtype)


def matmul(x, w_stack, layer, out_dtype, tm, tn):
    m, k = x.shape
    n = w_stack.shape[2]
    tm, tn = min(tm, m), min(tn, n)
    assert m % tm == 0 and n % tn == 0, (m, n, tm, tn)
    return pl.pallas_call(
        _mm_kernel,
        grid=(m // tm, n // tn),
        in_specs=[pl.BlockSpec((tm, k), lambda i, j: (i, 0)),
                  pl.BlockSpec((None, k, tn), lambda i, j: (layer, 0, j))],
        out_specs=pl.BlockSpec((tm, tn), lambda i, j: (i, j)),
        out_shape=jax.ShapeDtypeStruct((m, n), out_dtype),
        compiler_params=_params("parallel", "parallel"),
        name="matmul",
    )(x, w_stack)


def _mm_f32w_kernel(*refs):
    *x_refs, w_ref, o_ref, wb_ref = refs

    @pl.when(pl.program_id(1) == 0)
    def _():
        wb_ref[...] = w_ref[...].astype(BF16)

    acc, k0 = None, 0
    for x_ref in x_refs:
        k = x_ref.shape[1]
        part = _dot(x_ref[...], wb_ref[k0:k0 + k, :])
        acc = part if acc is None else acc + part
        k0 += k
    o_ref[...] = acc.astype(o_ref.dtype)


def matmul_f32w(xs, w_stack, layer, out_dtype, tm, tn):
    m = xs[0].shape[0]
    _, k, n = w_stack.shape
    assert sum(x.shape[1] for x in xs) == k
    tm, tn = min(tm, m), min(tn, n)
    assert m % tm == 0 and n % tn == 0, (m, n, tm, tn)
    return pl.pallas_call(
        _mm_f32w_kernel,
        grid=(n // tn, m // tm),
        in_specs=([pl.BlockSpec((tm, x.shape[1]), lambda j, i: (i, 0)) for x in xs]
                  + [pl.BlockSpec((None, k, tn), lambda j, i: (layer, 0, j))]),
        out_specs=pl.BlockSpec((tm, tn), lambda j, i: (i, j)),
        out_shape=jax.ShapeDtypeStruct((m, n), out_dtype),
        scratch_shapes=[pltpu.VMEM((k, tn), BF16)],
        compiler_params=_params("parallel", "arbitrary"),
        name="matmul_f32w",
    )(*xs, w_stack)


def _ln_kernel(x_ref, mix_ref, g_ref, b_ref, xo_ref, xb_ref):
    h = DEEPNORM_ALPHA * x_ref[...] + mix_ref[...]
    mu = jnp.mean(h, axis=-1, keepdims=True)
    d = h - mu
    var = jnp.mean(d * d, axis=-1, keepdims=True)
    y = d * lax.rsqrt(var + LN_EPS) * g_ref[...] + b_ref[...]
    xo_ref[...] = y
    xb_ref[...] = y.astype(BF16)


def residual_layer_norm(x, mix, g, b, tm=256):
    s, d = x.shape
    tm = min(tm, s)
    row = pl.BlockSpec((tm, d), lambda i: (i, 0))
    vec = pl.BlockSpec((1, d), lambda i: (0, 0))
    return pl.pallas_call(
        _ln_kernel,
        grid=(s // tm,),
        in_specs=[row, row, vec, vec],
        out_specs=[row, row],
        out_shape=[jax.ShapeDtypeStruct((s, d), F32), jax.ShapeDtypeStruct((s, d), BF16)],
        compiler_params=_params("parallel"),
        name="residual_layer_norm",
    )(x, mix, g.reshape(1, d), b.reshape(1, d))


def _hgrn_kernel(lbl_ref, q_ref, f_ref, i_ref, g_ref, gn_ref, o_ref,
                 st_ref, qs_s, k_s, b_s, o_s, u_s, stb_s, *, layer, ts):
    dk = MIX_HEAD_DIM
    c = HGRN_SUB

    @pl.when(pl.program_id(1) == 0)
    def _():
        st_ref[...] = jnp.zeros_like(st_ref)

    lg = lbl_ref[...]
    e = jnp.exp(lg - jnp.max(lg, axis=0, keepdims=True))
    sm = e / jnp.sum(e, axis=0, keepdims=True)
    cum = sm[0:1]
    for l in range(1, layer + 1):
        cum = cum + sm[l:l + 1]
    lb = cum - sm[0:1]

    fz = f_ref[...]
    ls_pos, ls_neg = _log_sigmoid_parts(fz)
    la = jnp.log(jnp.maximum(lb, LB_TINY))
    lc = jnp.log1p(-lb) + ls_pos
    log_f = jnp.maximum(la, lc) + jnp.log(1.0 + jnp.exp(_neg_abs(la - lc)))
    q = q_ref[...]
    qs_s[...] = q * jax.nn.sigmoid(q)
    k_s[...] = (1.0 - lb) * jnp.exp(ls_neg)

    r = lax.broadcasted_iota(jnp.int32, (ts, ts), 0)
    cc = lax.broadcasted_iota(jnp.int32, (ts, ts), 1)
    tri = jnp.where((r // c == cc // c) & (cc <= r), 1.0, 0.0).astype(BF16)
    hi, lo = _split_bf16(log_f)
    b_s[...] = _dot(tri, hi) + _dot(tri, lo)

    ones = jnp.ones((dk, dk), BF16)
    t_idx = lax.broadcasted_iota(jnp.int32, (c, dk), 0)

    n_sub = ts // c

    def sub_rows(j):
        return pl.ds(pl.multiple_of(j * c, c), c)

    @functools.partial(lax.fori_loop, 0, n_sub, init_val=None, unroll=8)
    def _(j, _):
        rows = sub_rows(j)
        b = b_s[rows, :]
        kh = k_s[rows, :] * jnp.exp(b[c - 1:c, :] - b)
        u_s[j] = _dot_tn(i_ref[rows, :].astype(BF16), kh.astype(BF16))

    def advance(j, st):
        stb_s[j] = st.astype(BF16)
        return st * jnp.exp(b_s[pl.ds(j * c + c - 1, 1), :]) + u_s[j]

    st_ref[...] = lax.fori_loop(0, n_sub, advance, st_ref[...], unroll=4)

    @functools.partial(lax.fori_loop, 0, n_sub, init_val=None, unroll=8)
    def _(j, _):
        rows = sub_rows(j)
        qs, kk, b, vv = qs_s[rows, :], k_s[rows, :], b_s[rows, :], i_ref[rows, :]
        o_inter = _dot_nt((qs * jnp.exp(b)).astype(BF16), stb_s[j])
        prods = []
        for s in range(c):
            p = qs * kk[s:s + 1, :] * jnp.exp(b - b[s:s + 1, :])
            prods.append(jnp.where(t_idx >= s, p, 0.0).astype(BF16))
        scores = _dot(jnp.concatenate(prods, axis=0), ones)
        o_diag = scores[0:c] * vv[0:1, :]
        for s in range(1, c):
            o_diag = o_diag + scores[s * c:(s + 1) * c] * vv[s:s + 1, :]
        o_s[rows, :] = o_inter + o_diag

    of = o_s[...]
    of = of * lax.rsqrt(jnp.mean(of * of, axis=-1, keepdims=True) + RMS_EPS)
    g = g_ref[...]
    o_ref[...] = (of * gn_ref[...] * (g * jax.nn.sigmoid(g))).astype(o_ref.dtype)


def hgrn2_mix(y, lb_logits, onorm_g, layer, n_heads, ts=512):
    s = y.shape[0]
    dk = MIX_HEAD_DIM
    ts = min(ts, s)
    n_layers = lb_logits.shape[0]

    def col(group):
        return pl.BlockSpec((ts, dk), lambda h, i, group=group: (i, group * n_heads + h))

    return pl.pallas_call(
        functools.partial(_hgrn_kernel, layer=layer, ts=ts),
        grid=(n_heads, s // ts),
        in_specs=[pl.BlockSpec((n_layers, dk), lambda h, i: (0, h)),
                  col(0), col(1), col(2), col(3),
                  pl.BlockSpec((1, dk), lambda h, i: (0, h))],
        out_specs=pl.BlockSpec((ts, dk), lambda h, i: (i, h)),
        out_shape=jax.ShapeDtypeStruct((s, n_heads * dk), BF16),
        scratch_shapes=([pltpu.VMEM((dk, dk), F32)] + [pltpu.VMEM((ts, dk), F32)] * 4
                        + [pltpu.VMEM((ts // HGRN_SUB, dk, dk), F32), pltpu.VMEM((ts // HGRN_SUB, dk, dk), BF16)]),
        compiler_params=_params("parallel", "arbitrary"),
        name="hgrn2_mix",
    )(lb_logits, y, y, y, y, onorm_g.reshape(1, -1))


def _sb_kernel(q_ref, k_ref, v_ref, o_ref, acc_ref, rest_ref, *, t, g):
    d = MIX_HEAD_DIM
    scale2 = d ** -0.5 * LOG2_E
    qi = pl.program_id(1)
    jj = lax.broadcasted_iota(jnp.int32, (t, t + d), 0)
    ss = lax.broadcasted_iota(jnp.int32, (t, t + d), 1)
    later = jnp.where((jj > ss) | (ss >= t), 1.0, 0.0).astype(BF16)
    causal = lax.broadcasted_iota(jnp.int32, (t, t), 1) < lax.broadcasted_iota(jnp.int32, (t, t), 0)

    def key_blocks(kb, on_diagonal):
        rows = pl.ds(pl.multiple_of(kb * t, t), t)
        heads = [slice(h * d, (h + 1) * d) for h in range(g)]
        z2 = [_dot_nt(q_ref[:, c], k_ref[rows, c]) * scale2 for c in heads]
        log_beta, log_rest = [], []
        for z in z2:
            lb = jnp.minimum(z, 0.0) - jnp.log2(1.0 + jnp.exp2(_neg_abs(z)))
            lr = lb - z
            log_beta.append(lb)
            log_rest.append(jnp.where(causal, lr, 0.0) if on_diagonal else lr)
        sums = [_dot(lr.astype(BF16), later) for lr in log_rest]
        for c, lb, sm in zip(heads, log_beta, sums):
            if on_diagonal:
                w = jnp.where(causal, jnp.exp2(lb + sm[:, :t]), 0.0)
                acc_ref[:, c] = _dot(w.astype(BF16), v_ref[rows, c])
                rest_ref[:, c] = sm[:, t:]
            else:
                rest = rest_ref[:, c]
                w = jnp.exp2(lb + sm[:, :t] + jnp.concatenate([rest] * (t // d), axis=1))
                acc_ref[:, c] += _dot(w.astype(BF16), v_ref[rows, c])
                rest_ref[:, c] = rest + sm[:, t:]

    def any_weight_left():
        return jnp.max(rest_ref[...]) > SB_UNDERFLOW_LOG2

    key_blocks(qi, True)

    def earlier_block(carry):
        i, _ = carry
        key_blocks(qi - 1 - i, False)
        return i + 1, any_weight_left()

    lax.while_loop(lambda carry: (carry[0] < qi) & carry[1], earlier_block, (jnp.int32(0), any_weight_left()))

    o_ref[...] = acc_ref[...].astype(o_ref.dtype)


def stick_breaking_mix(qy, kv, n_heads, t=256, g=8):
    s = qy.shape[0]
    d = MIX_HEAD_DIM
    t = min(t, s)
    g = min(g, n_heads)
    assert t % d == 0 and s % t == 0 and n_heads % g == 0
    return pl.pallas_call(
        functools.partial(_sb_kernel, t=t, g=g),
        grid=(n_heads // g, s // t),
        in_specs=[pl.BlockSpec((t, g * d), lambda h, i: (i, h)),
                  pl.BlockSpec((s, g * d), lambda h, i: (0, h), pipeline_mode=pl.Buffered(1)),
                  pl.BlockSpec((s, g * d), lambda h, i: (0, n_heads // g + h), pipeline_mode=pl.Buffered(1))],
        out_specs=pl.BlockSpec((t, g * d), lambda h, i: (i, h)),
        out_shape=jax.ShapeDtypeStruct((s, n_heads * d), BF16),
        scratch_shapes=[pltpu.VMEM((t, g * d), F32)] * 2,
        compiler_params=_params("parallel", "parallel"),
        name="stick_breaking_mix",
    )(qy, kv, kv)


def _mem_kernel(q_ref, k_ref, v_ref, o_ref):
    hd = q_ref.shape[-1]
    s = _dot_nt(q_ref[...].astype(BF16), k_ref[...]) * (hd ** -0.5)
    e = jnp.exp(s - jnp.max(s, axis=-1, keepdims=True))
    p = e / jnp.sum(e, axis=-1, keepdims=True)
    o_ref[...] = _dot(p.astype(BF16), v_ref[...]).astype(o_ref.dtype)


def memory_mix(qy, q_offset, mem_kv, tq=512):
    s = qy.shape[0]
    n_mem, w2 = mem_kv.shape
    hd = w2 // 2 // MEM_HEADS
    tq = min(tq, s)
    q0 = q_offset // hd
    return pl.pallas_call(
        _mem_kernel,
        grid=(s // tq, MEM_HEADS),
        in_specs=[pl.BlockSpec((tq, hd), lambda i, h: (i, q0 + h)),
                  pl.BlockSpec((n_mem, hd), lambda i, h: (0, h)),
                  pl.BlockSpec((n_mem, hd), lambda i, h: (0, MEM_HEADS + h))],
        out_specs=pl.BlockSpec((tq, hd), lambda i, h: (i, h)),
        out_shape=jax.ShapeDtypeStruct((s, MEM_HEADS * hd), BF16),
        compiler_params=_params("parallel", "parallel"),
        name="memory_mix",
    )(qy, mem_kv, mem_kv)


def _ffn_in_kernel(x_ref, wg_ref, wu_ref, cw_ref, cb_ref, o_ref, tail_ref, wgb_ref, wub_ref):
    tm = x_ref.shape[0]

    @pl.when(pl.program_id(1) == 0)
    def _():
        tail_ref[...] = jnp.zeros_like(tail_ref)
        wgb_ref[...] = wg_ref[...].astype(BF16)
        wub_ref[...] = wu_ref[...].astype(BF16)

    x = x_ref[...]
    gate = _dot(x, wgb_ref[...])
    up = _dot(x, wub_ref[...])
    tail = tail_ref[...]
    row = lax.broadcasted_iota(jnp.int32, gate.shape, 0)
    g1 = jnp.where(row == 0, tail[SUBLANES - 1:SUBLANES, :], pltpu.roll(gate, 1, 0))
    g2 = jnp.where(row == 0, tail[SUBLANES - 2:SUBLANES - 1, :],
                   jnp.where(row == 1, tail[SUBLANES - 1:SUBLANES, :], pltpu.roll(gate, 2, 0)))
    cw = cw_ref[...]
    conv = cb_ref[...] + cw[0:1, :] * g2
    conv = conv + cw[1:2, :] * g1
    conv = conv + cw[2:3, :] * gate
    tail_ref[...] = gate[tm - SUBLANES:tm, :]
    o_ref[...] = (conv * jax.nn.sigmoid(conv) * up).astype(o_ref.dtype)


def conv_glu_in(x, w_in, conv_w, conv_b, layer, tm=1024, tn=256):
    s, d = x.shape
    d_ff = w_in.shape[2] // 2
    tm = min(tm, s)
    tn = min(tn, d_ff)
    assert s % tm == 0 and d_ff % tn == 0 and CONV_WIDTH == 3
    nt = d_ff // tn
    return pl.pallas_call(
        _ffn_in_kernel,
        grid=(nt, s // tm),
        in_specs=[pl.BlockSpec((tm, d), lambda j, i: (i, 0)),
                  pl.BlockSpec((None, d, tn), lambda j, i: (layer, 0, j)),
                  pl.BlockSpec((None, d, tn), lambda j, i: (layer, 0, nt + j)),
                  pl.BlockSpec((None, CONV_WIDTH, tn), lambda j, i: (layer, 0, j)),
                  pl.BlockSpec((None, 1, tn), lambda j, i: (layer, 0, j))],
        out_specs=pl.BlockSpec((tm, tn), lambda j, i: (i, j)),
        out_shape=jax.ShapeDtypeStruct((s, d_ff), BF16),
        scratch_shapes=[pltpu.VMEM((SUBLANES, tn), F32), pltpu.VMEM((d, tn), BF16), pltpu.VMEM((d, tn), BF16)],
        compiler_params=_params("parallel", "arbitrary"),
        name="conv_glu_in",
    )(x, w_in, w_in, conv_w, conv_b.reshape(conv_b.shape[0], 1, d_ff))


def kernel(x, mem, a_w_in, hgrn_lb_logits, a_onorm_g, b_w_in, w_kv_shared, w_mem_kv, w_o,
           ffn_w_in, ffn_conv_w, ffn_conv_b, ffn_w_out, ln_g, ln_b):
    batch, seq, d_model = x.shape
    assert batch == 1
    mix_width = a_onorm_g.shape[1]
    n_heads = mix_width // MIX_HEAD_DIM
    bf = lambda t: t.astype(BF16)

    xf = x[0]
    xb = bf(xf)
    mem_b = bf(mem[0])
    w_out_b = bf(ffn_w_out)
    kv = None
    for layer in range(DEPTH):
        mem_kv = matmul_f32w([mem_b], w_mem_kv, layer, BF16, 256, 512)
        if layer < N_A_LAYERS:
            y = matmul_f32w([xb], a_w_in, layer, F32, 1024, 512)
            o = hgrn2_mix(y, hgrn_lb_logits, a_onorm_g[layer], layer, n_heads)
            m = memory_mix(y, 4 * mix_width, mem_kv)
        else:
            if layer == N_A_LAYERS:
                kv = matmul_f32w([xb], w_kv_shared[None], 0, BF16, 1024, 512)
            y = matmul_f32w([xb], b_w_in, layer - N_A_LAYERS, BF16, 1024, 512)
            o = stick_breaking_mix(y, kv, n_heads)
            m = memory_mix(y, mix_width, mem_kv)
        mix = matmul_f32w([o, m], w_o, layer, F32, 1024, 512)
        xf, xb = residual_layer_norm(xf, mix, ln_g[layer, 0], ln_b[layer, 0])
        act = conv_glu_in(xb, ffn_w_in, ffn_conv_w, ffn_conv_b, layer)
        ffn = matmul(act, w_out_b, layer, F32, 512, 512)
        xf, xb = residual_layer_norm(xf, ffn, ln_g[layer, 1], ln_b[layer, 1])
    return xf[None]
```

```python
import functools

import jax
import jax.numpy as jnp
from jax import lax
from jax.experimental import pallas as pl
from jax.experimental.pallas import tpu as pltpu

DEPTH = 4
N_A_LAYERS = DEPTH // 2
MIX_HEAD_DIM = 128
MEM_HEADS = 4
CONV_WIDTH = 3
LN_EPS = 1e-5
RMS_EPS = 1e-6
LB_TINY = 1e-30
DEEPNORM_ALPHA = (2 * DEPTH) ** 0.25
LOG2_E = 1.4426950408889634
SB_UNDERFLOW_LOG2 = -160.0

VMEM_LIMIT_BYTES = 56 * 1024 * 1024
SUBLANES = 8
HGRN_SUB = 16

F32 = jnp.float32
BF16 = jnp.bfloat16


def _params(*semantics):
    return pltpu.CompilerParams(dimension_semantics=semantics, vmem_limit_bytes=VMEM_LIMIT_BYTES)


def _dot(a, b):
    return jnp.dot(a, b, preferred_element_type=F32)


def _dot_nt(a, b):
    return lax.dot_general(a, b, (((1,), (1,)), ((), ())), preferred_element_type=F32)


def _dot_tn(a, b):
    return lax.dot_general(a, b, (((0,), (0,)), ((), ())), preferred_element_type=F32)


def _neg_abs(x):
    bits = lax.bitcast_convert_type(x, jnp.uint32) | jnp.uint32(0x80000000)
    return lax.bitcast_convert_type(bits, F32)


def _log_sigmoid_parts(z):
    sp = jnp.log(1.0 + jnp.exp(_neg_abs(z)))
    return jnp.minimum(z, 0.0) - sp, -jnp.maximum(z, 0.0) - sp


def _split_bf16(x):
    hi = x.astype(BF16)
    lo = (x - hi.astype(F32)).astype(BF16)
    return hi, lo


def _mm_kernel(x_ref, w_ref, o_ref):
    o_ref[...] = _dot(x_ref[...], w_ref[...]).astype(o_ref.dtype)


def matmul(x, w_stack, layer, out_dtype, tm, tn):
    m, k = x.shape
    n = w_stack.shape[2]
    tm, tn = min(tm, m), min(tn, n)
    assert m % tm == 0 and n % tn == 0, (m, n, tm, tn)
    return pl.pallas_call(
        _mm_kernel,
        grid=(m // tm, n // tn),
        in_specs=[pl.BlockSpec((tm, k), lambda i, j: (i, 0)),
                  pl.BlockSpec((None, k, tn), lambda i, j: (layer, 0, j))],
        out_specs=pl.BlockSpec((tm, tn), lambda i, j: (i, j)),
        out_shape=jax.ShapeDtypeStruct((m, n), out_dtype),
        compiler_params=_params("parallel", "parallel"),
        name="matmul",
    )(x, w_stack)


def _mm_f32w_kernel(*refs):
    *x_refs, w_ref, o_ref, wb_ref = refs

    @pl.when(pl.program_id(1) == 0)
    def _():
        wb_ref[...] = w_ref[...].astype(BF16)

    acc, k0 = None, 0
    for x_ref in x_refs:
        k = x_ref.shape[1]
        part = _dot(x_ref[...], wb_ref[k0:k0 + k, :])
        acc = part if acc is None else acc + part
        k0 += k
    o_ref[...] = acc.astype(o_ref.dtype)


def matmul_f32w(xs, w_stack, layer, out_dtype, tm, tn):
    m = xs[0].shape[0]
    _, k, n = w_stack.shape
    assert sum(x.shape[1] for x in xs) == k
    tm, tn = min(tm, m), min(tn, n)
    assert m % tm == 0 and n % tn == 0, (m, n, tm, tn)
    return pl.pallas_call(
        _mm_f32w_kernel,
        grid=(n // tn, m // tm),
        in_specs=([pl.BlockSpec((tm, x.shape[1]), lambda j, i: (i, 0)) for x in xs]
                  + [pl.BlockSpec((None, k, tn), lambda j, i: (layer, 0, j))]),
        out_specs=pl.BlockSpec((tm, tn), lambda j, i: (i, j)),
        out_shape=jax.ShapeDtypeStruct((m, n), out_dtype),
        scratch_shapes=[pltpu.VMEM((k, tn), BF16)],
        compiler_params=_params("parallel", "arbitrary"),
        name="matmul_f32w",
    )(*xs, w_stack)


def _ln_kernel(x_ref, mix_ref, g_ref, b_ref, xo_ref, xb_ref):
    h = DEEPNORM_ALPHA * x_ref[...] + mix_ref[...].astype(F32)
    mu = jnp.mean(h, axis=-1, keepdims=True)
    d = h - mu
    var = jnp.mean(d * d, axis=-1, keepdims=True)
    y = d * lax.rsqrt(var + LN_EPS) * g_ref[...] + b_ref[...]
    xo_ref[...] = y
    xb_ref[...] = y.astype(BF16)


def residual_layer_norm(x, mix, g, b, tm=256):
    s, d = x.shape
    tm = min(tm, s)
    row = pl.BlockSpec((tm, d), lambda i: (i, 0))
    vec = pl.BlockSpec((1, d), lambda i: (0, 0))
    return pl.pallas_call(
        _ln_kernel,
        grid=(s // tm,),
        in_specs=[row, row, vec, vec],
        out_specs=[row, row],
        out_shape=[jax.ShapeDtypeStruct((s, d), F32), jax.ShapeDtypeStruct((s, d), BF16)],
        compiler_params=_params("parallel"),
        name="residual_layer_norm",
    )(x, mix, g.reshape(1, d), b.reshape(1, d))


def _hgrn_kernel(lbl_ref, q_ref, f_ref, i_ref, g_ref, gn_ref, tri_ref, o_ref,
                 st_ref, qs_s, k_s, b_s, o_s, u_s, stb_s, *, layer, ts):
    dk = MIX_HEAD_DIM
    c = HGRN_SUB

    @pl.when(pl.program_id(1) == 0)
    def _():
        st_ref[...] = jnp.zeros_like(st_ref)

    lg = lbl_ref[...]
    e = jnp.exp(lg - jnp.max(lg, axis=0, keepdims=True))
    sm = e / jnp.sum(e, axis=0, keepdims=True)
    cum = sm[0:1]
    for l in range(1, layer + 1):
        cum = cum + sm[l:l + 1]
    lb = cum - sm[0:1]

    fz = f_ref[...]
    ls_pos, ls_neg = _log_sigmoid_parts(fz)
    la = jnp.log(jnp.maximum(lb, LB_TINY))
    lc = jnp.log1p(-lb) + ls_pos
    log_f = jnp.maximum(la, lc) + jnp.log(1.0 + jnp.exp(_neg_abs(la - lc)))
    q = q_ref[...]
    qs_s[...] = q * jax.nn.sigmoid(q)
    k_s[...] = (1.0 - lb) * jnp.exp(ls_neg)

    tri = tri_ref[...]
    tb = tri.shape[0]
    hi, lo = _split_bf16(log_f)
    for r0 in range(0, ts, tb):
        b_s[r0:r0 + tb, :] = _dot(tri, hi[r0:r0 + tb]) + _dot(tri, lo[r0:r0 + tb])

    ones = jnp.ones((dk, dk), BF16)
    t_idx = lax.broadcasted_iota(jnp.int32, (c, dk), 0)

    n_sub = ts // c

    def sub_rows(j):
        return pl.ds(pl.multiple_of(j * c, c), c)

    @functools.partial(lax.fori_loop, 0, n_sub, init_val=None, unroll=True)
    def _(j, _):
        rows = sub_rows(j)
        b = b_s[rows, :]
        kh = k_s[rows, :] * jnp.exp(b[c - 1:c, :] - b)
        u_s[j] = _dot_tn(i_ref[rows, :].astype(BF16), kh.astype(BF16))

    def advance(j, st):
        stb_s[j] = st.astype(BF16)
        return st * jnp.exp(b_s[pl.ds(j * c + c - 1, 1), :]) + u_s[j]

    st_ref[...] = lax.fori_loop(0, n_sub, advance, st_ref[...], unroll=8)

    @functools.partial(lax.fori_loop, 0, n_sub, init_val=None, unroll=16)
    def _(j, _):
        rows = sub_rows(j)
        qs, kk, b, vv = qs_s[rows, :], k_s[rows, :], b_s[rows, :], i_ref[rows, :]
        o_inter = _dot_nt((qs * jnp.exp(b)).astype(BF16), stb_s[j])
        h = c // 2
        prods = []
        for s in range(c):
            lo = 0 if s < h else h
            p = qs[lo:] * kk[s:s + 1, :] * jnp.exp(b[lo:] - b[s:s + 1, :])
            prods.append(jnp.where(t_idx[lo:] >= s, p, 0.0))
        scores = _dot(jnp.concatenate(prods, axis=0).astype(BF16), ones)
        o_top = scores[0:h] * vv[0:1, :]
        o_bot = scores[h:c] * vv[0:1, :]
        for s in range(1, h):
            o_top = o_top + scores[s * c:s * c + h] * vv[s:s + 1, :]
            o_bot = o_bot + scores[s * c + h:(s + 1) * c] * vv[s:s + 1, :]
        for s in range(h, c):
            r0 = h * c + (s - h) * h
            o_bot = o_bot + scores[r0:r0 + h] * vv[s:s + 1, :]
        o_s[rows, :] = o_inter + jnp.concatenate([o_top, o_bot], axis=0)

    of = o_s[...]
    of = of * lax.rsqrt(jnp.mean(of * of, axis=-1, keepdims=True) + RMS_EPS)
    g = g_ref[...]
    o_ref[...] = (of * gn_ref[...] * (g * jax.nn.sigmoid(g))).astype(o_ref.dtype)


def hgrn2_mix(y, lb_logits, onorm_g, layer, n_heads, ts=512):
    s = y.shape[0]
    dk = MIX_HEAD_DIM
    ts = min(ts, s)
    n_layers = lb_logits.shape[0]

    def col(group):
        return pl.BlockSpec((ts, dk), lambda h, i, group=group: (i, group * n_heads + h))

    tb = min(MIX_HEAD_DIM, ts)
    r = lax.broadcasted_iota(jnp.int32, (tb, tb), 0)
    cc = lax.broadcasted_iota(jnp.int32, (tb, tb), 1)
    tri = ((r // HGRN_SUB == cc // HGRN_SUB) & (cc <= r)).astype(BF16)

    return pl.pallas_call(
        functools.partial(_hgrn_kernel, layer=layer, ts=ts),
        grid=(n_heads, s // ts),
        in_specs=[pl.BlockSpec((n_layers, dk), lambda h, i: (0, h)),
                  col(0), col(1), col(2), col(3),
                  pl.BlockSpec((1, dk), lambda h, i: (0, h)),
                  pl.BlockSpec((tb, tb), lambda h, i: (0, 0))],
        out_specs=pl.BlockSpec((ts, dk), lambda h, i: (i, h)),
        out_shape=jax.ShapeDtypeStruct((s, n_heads * dk), BF16),
        scratch_shapes=([pltpu.VMEM((dk, dk), F32)] + [pltpu.VMEM((ts, dk), F32)] * 4
                        + [pltpu.VMEM((ts // HGRN_SUB, dk, dk), F32), pltpu.VMEM((ts // HGRN_SUB, dk, dk), BF16)]),
        compiler_params=_params("parallel", "arbitrary"),
        name="hgrn2_mix",
    )(lb_logits, y, y, y, y, onorm_g.reshape(1, -1), tri)


def _sb_kernel(q_ref, k_ref, v_ref, o_ref, acc_ref, rest_ref, *, t, g):
    d = MIX_HEAD_DIM
    scale2 = d ** -0.5 * LOG2_E
    qi = pl.program_id(1)
    jj = lax.broadcasted_iota(jnp.int32, (t, t + d), 0)
    ss = lax.broadcasted_iota(jnp.int32, (t, t + d), 1)
    later = jnp.where((jj > ss) | (ss >= t), 1.0, 0.0).astype(BF16)
    causal = lax.broadcasted_iota(jnp.int32, (t, t), 1) < lax.broadcasted_iota(jnp.int32, (t, t), 0)

    def key_blocks(kb, on_diagonal):
        rows = pl.ds(pl.multiple_of(kb * t, t), t)
        heads = [slice(h * d, (h + 1) * d) for h in range(g)]
        z2 = [_dot_nt(q_ref[:, c], k_ref[rows, c]) * scale2 for c in heads]
        log_beta, log_rest = [], []
        for z in z2:
            lb = jnp.minimum(z, 0.0) - jnp.log2(1.0 + jnp.exp2(_neg_abs(z)))
            lr = lb - z
            log_beta.append(lb)
            log_rest.append(jnp.where(causal, lr, 0.0) if on_diagonal else lr)
        sums = [_dot(lr.astype(BF16), later) for lr in log_rest]
        for c, lb, sm in zip(heads, log_beta, sums):
            if on_diagonal:
                w = jnp.where(causal, jnp.exp2(lb + sm[:, :t]), 0.0)
                acc_ref[:, c] = _dot(w.astype(BF16), v_ref[rows, c])
                rest_ref[:, c] = sm[:, t:]
            else:
                rest = rest_ref[:, c]
                w = jnp.exp2(lb + sm[:, :t] + jnp.concatenate([rest] * (t // d), axis=1))
                acc_ref[:, c] += _dot(w.astype(BF16), v_ref[rows, c])
                rest_ref[:, c] = rest + sm[:, t:]

    def any_weight_left():
        return jnp.max(rest_ref[...]) > SB_UNDERFLOW_LOG2

    key_blocks(qi, True)

    def earlier_block(carry):
        i, _ = carry
        key_blocks(qi - 1 - i, False)
        return i + 1, any_weight_left()

    lax.while_loop(lambda carry: (carry[0] < qi) & carry[1], earlier_block, (jnp.int32(0), any_weight_left()))

    o_ref[...] = acc_ref[...].astype(o_ref.dtype)


def stick_breaking_mix(qy, kv, n_heads, t=256, g=8):
    s = qy.shape[0]
    d = MIX_HEAD_DIM
    t = min(t, s)
    g = min(g, n_heads)
    assert t % d == 0 and s % t == 0 and n_heads % g == 0
    return pl.pallas_call(
        functools.partial(_sb_kernel, t=t, g=g),
        grid=(n_heads // g, s // t),
        in_specs=[pl.BlockSpec((t, g * d), lambda h, i: (i, h)),
                  pl.BlockSpec((s, g * d), lambda h, i: (0, h), pipeline_mode=pl.Buffered(1)),
                  pl.BlockSpec((s, g * d), lambda h, i: (0, n_heads // g + h), pipeline_mode=pl.Buffered(1))],
        out_specs=pl.BlockSpec((t, g * d), lambda h, i: (i, h)),
        out_shape=jax.ShapeDtypeStruct((s, n_heads * d), BF16),
        scratch_shapes=[pltpu.VMEM((t, g * d), F32)] * 2,
        compiler_params=_params("parallel", "parallel"),
        name="stick_breaking_mix",
    )(qy, kv, kv)


def _mem_kernel(q_ref, k_ref, v_ref, o_ref):
    hd = q_ref.shape[-1]
    s = _dot_nt(q_ref[...].astype(BF16), k_ref[...]) * (hd ** -0.5)
    e = jnp.exp(s - jnp.max(s, axis=-1, keepdims=True))
    p = e / jnp.sum(e, axis=-1, keepdims=True)
    o_ref[...] = _dot(p.astype(BF16), v_ref[...]).astype(o_ref.dtype)


def memory_mix(qy, q_offset, mem_kv, tq=512):
    s = qy.shape[0]
    n_mem, w2 = mem_kv.shape
    hd = w2 // 2 // MEM_HEADS
    tq = min(tq, s)
    q0 = q_offset // hd
    return pl.pallas_call(
        _mem_kernel,
        grid=(s // tq, MEM_HEADS),
        in_specs=[pl.BlockSpec((tq, hd), lambda i, h: (i, q0 + h)),
                  pl.BlockSpec((n_mem, hd), lambda i, h: (0, h)),
                  pl.BlockSpec((n_mem, hd), lambda i, h: (0, MEM_HEADS + h))],
        out_specs=pl.BlockSpec((tq, hd), lambda i, h: (i, h)),
        out_shape=jax.ShapeDtypeStruct((s, MEM_HEADS * hd), BF16),
        compiler_params=_params("parallel", "parallel"),
        name="memory_mix",
    )(qy, mem_kv, mem_kv)


def _ffn_in_kernel(x_ref, wg_ref, wu_ref, cw_ref, cb_ref, o_ref, tail_ref, wgb_ref, wub_ref):
    tm = x_ref.shape[0]

    @pl.when(pl.program_id(1) == 0)
    def _():
        tail_ref[...] = jnp.zeros_like(tail_ref)
        wgb_ref[...] = wg_ref[...].astype(BF16)
        wub_ref[...] = wu_ref[...].astype(BF16)

    x = x_ref[...]
    gate = _dot(x, wgb_ref[...])
    up = _dot(x, wub_ref[...])
    tail = tail_ref[...]
    row = lax.broadcasted_iota(jnp.int32, gate.shape, 0)
    g1 = jnp.where(row == 0, tail[SUBLANES - 1:SUBLANES, :], pltpu.roll(gate, 1, 0))
    g2 = jnp.where(row == 0, tail[SUBLANES - 2:SUBLANES - 1, :],
                   jnp.where(row == 1, tail[SUBLANES - 1:SUBLANES, :], pltpu.roll(gate, 2, 0)))
    cw = cw_ref[...]
    conv = cb_ref[...] + cw[0:1, :] * g2
    conv = conv + cw[1:2, :] * g1
    conv = conv + cw[2:3, :] * gate
    tail_ref[...] = gate[tm - SUBLANES:tm, :]
    o_ref[...] = (conv * jax.nn.sigmoid(conv) * up).astype(o_ref.dtype)


def conv_glu_in(x, w_in, conv_w, conv_b, layer, tm=1024, tn=256):
    s, d = x.shape
    d_ff = w_in.shape[2] // 2
    tm = min(tm, s)
    tn = min(tn, d_ff)
    assert s % tm == 0 and d_ff % tn == 0 and CONV_WIDTH == 3
    nt = d_ff // tn
    return pl.pallas_call(
        _ffn_in_kernel,
        grid=(nt, s // tm),
        in_specs=[pl.BlockSpec((tm, d), lambda j, i: (i, 0)),
                  pl.BlockSpec((None, d, tn), lambda j, i: (layer, 0, j)),
                  pl.BlockSpec((None, d, tn), lambda j, i: (layer, 0, nt + j)),
                  pl.BlockSpec((None, CONV_WIDTH, tn), lambda j, i: (layer, 0, j)),
                  pl.BlockSpec((None, 1, tn), lambda j, i: (layer, 0, j))],
        out_specs=pl.BlockSpec((tm, tn), lambda j, i: (i, j)),
        out_shape=jax.ShapeDtypeStruct((s, d_ff), BF16),
        scratch_shapes=[pltpu.VMEM((SUBLANES, tn), F32), pltpu.VMEM((d, tn), BF16), pltpu.VMEM((d, tn), BF16)],
        compiler_params=_params("parallel", "arbitrary"),
        name="conv_glu_in",
    )(x, w_in, w_in, conv_w, conv_b.reshape(conv_b.shape[0], 1, d_ff))


def kernel(x, mem, a_w_in, hgrn_lb_logits, a_onorm_g, b_w_in, w_kv_shared, w_mem_kv, w_o,
           ffn_w_in, ffn_conv_w, ffn_conv_b, ffn_w_out, ln_g, ln_b):
    batch, seq, d_model = x.shape
    assert batch == 1
    mix_width = a_onorm_g.shape[1]
    n_heads = mix_width // MIX_HEAD_DIM
    bf = lambda t: t.astype(BF16)

    xf = x[0]
    xb = bf(xf)
    mem_b = bf(mem[0])
    w_out_b = bf(ffn_w_out)
    kv = None
    for layer in range(DEPTH):
        mem_kv = matmul_f32w([mem_b], w_mem_kv, layer, BF16, 256, 512)
        if layer < N_A_LAYERS:
            y = matmul_f32w([xb], a_w_in, layer, F32, 1024, 512)
            o = hgrn2_mix(y, hgrn_lb_logits, a_onorm_g[layer], layer, n_heads)
            m = memory_mix(y, 4 * mix_width, mem_kv)
        else:
            if layer == N_A_LAYERS:
                kv = matmul_f32w([xb], w_kv_shared[None], 0, BF16, 1024, 512)
            y = matmul_f32w([xb], b_w_in, layer - N_A_LAYERS, BF16, 1024, 512)
            o = stick_breaking_mix(y, kv, n_heads)
            m = memory_mix(y, mix_width, mem_kv)
        mix = matmul_f32w([o, m], w_o, layer, BF16, 1024, 512)
        xf, xb = residual_layer_norm(xf, mix, ln_g[layer, 0], ln_b[layer, 0])
        act = conv_glu_in(xb, ffn_w_in, ffn_conv_w, ffn_conv_b, layer)
        ffn = matmul(act, w_out_b, layer, BF16, 512, 512)
        xf, xb = residual_layer_norm(xf, ffn, ln_g[layer, 1], ln_b[layer, 1])
    return xf[None]
```

```python
import functools

import jax
import jax.numpy as jnp
from jax import lax
from jax.experimental import pallas as pl
from jax.experimental.pallas import tpu as pltpu

DEPTH = 4
N_A_LAYERS = DEPTH // 2
MIX_HEAD_DIM = 128
MEM_HEADS = 4
CONV_WIDTH = 3
LN_EPS = 1e-5
RMS_EPS = 1e-6
LB_TINY = 1e-30
DEEPNORM_ALPHA = (2 * DEPTH) ** 0.25
LOG2_E = 1.4426950408889634
SB_UNDERFLOW_LOG2 = -160.0

VMEM_LIMIT_BYTES = 56 * 1024 * 1024
SUBLANES = 8
HGRN_SUB = 16

F32 = jnp.float32
BF16 = jnp.bfloat16


def _params(*semantics):
    return pltpu.CompilerParams(dimension_semantics=semantics, vmem_limit_bytes=VMEM_LIMIT_BYTES)


def _dot(a, b):
    return jnp.dot(a, b, preferred_element_type=F32)


def _dot_nt(a, b):
    return lax.dot_general(a, b, (((1,), (1,)), ((), ())), preferred_element_type=F32)


def _dot_tn(a, b):
    return lax.dot_general(a, b, (((0,), (0,)), ((), ())), preferred_element_type=F32)


def _neg_abs(x):
    bits = lax.bitcast_convert_type(x, jnp.uint32) | jnp.uint32(0x80000000)
    return lax.bitcast_convert_type(bits, F32)


def _log_sigmoid_parts(z):
    sp = jnp.log(1.0 + jnp.exp(_neg_abs(z)))
    return jnp.minimum(z, 0.0) - sp, -jnp.maximum(z, 0.0) - sp


def _split_bf16(x):
    hi = x.astype(BF16)
    lo = (x - hi.astype(F32)).astype(BF16)
    return hi, lo


def _mm_kernel(x_ref, w_ref, o_ref):
    o_ref[...] = _dot(x_ref[...], w_ref[...]).astype(o_ref.dtype)


def matmul(x, w_stack, layer, out_dtype, tm, tn):
    m, k = x.shape
    n = w_stack.shape[2]
    tm, tn = min(tm, m), min(tn, n)
    assert m % tm == 0 and n % tn == 0, (m, n, tm, tn)
    return pl.pallas_call(
        _mm_kernel,
        grid=(m // tm, n // tn),
        in_specs=[pl.BlockSpec((tm, k), lambda i, j: (i, 0)),
                  pl.BlockSpec((None, k, tn), lambda i, j: (layer, 0, j))],
        out_specs=pl.BlockSpec((tm, tn), lambda i, j: (i, j)),
        out_shape=jax.ShapeDtypeStruct((m, n), out_dtype),
        compiler_params=_params("parallel", "parallel"),
        name="matmul",
    )(x, w_stack)


def _mm_f32w_kernel(*refs):
    *x_refs, w_ref, o_ref, wb_ref = refs

    @pl.when(pl.program_id(1) == 0)
    def _():
        wb_ref[...] = w_ref[...].astype(BF16)

    acc, k0 = None, 0
    for x_ref in x_refs:
        k = x_ref.shape[1]
        part = _dot(x_ref[...], wb_ref[k0:k0 + k, :])
        acc = part if acc is None else acc + part
        k0 += k
    o_ref[...] = acc.astype(o_ref.dtype)


def matmul_f32w(xs, w_stack, layer, out_dtype, tm, tn):
    m = xs[0].shape[0]
    _, k, n = w_stack.shape
    assert sum(x.shape[1] for x in xs) == k
    tm, tn = min(tm, m), min(tn, n)
    assert m % tm == 0 and n % tn == 0, (m, n, tm, tn)
    return pl.pallas_call(
        _mm_f32w_kernel,
        grid=(n // tn, m // tm),
        in_specs=([pl.BlockSpec((tm, x.shape[1]), lambda j, i: (i, 0)) for x in xs]
                  + [pl.BlockSpec((None, k, tn), lambda j, i: (layer, 0, j))]),
        out_specs=pl.BlockSpec((tm, tn), lambda j, i: (i, j)),
        out_shape=jax.ShapeDtypeStruct((m, n), out_dtype),
        scratch_shapes=[pltpu.VMEM((k, tn), BF16)],
        compiler_params=_params("parallel", "arbitrary"),
        name="matmul_f32w",
    )(*xs, w_stack)


def _ln_kernel(x_ref, mix_ref, g_ref, b_ref, xo_ref, xb_ref):
    h = DEEPNORM_ALPHA * x_ref[...] + mix_ref[...].astype(F32)
    mu = jnp.mean(h, axis=-1, keepdims=True)
    d = h - mu
    var = jnp.mean(d * d, axis=-1, keepdims=True)
    y = d * lax.rsqrt(var + LN_EPS) * g_ref[...] + b_ref[...]
    xo_ref[...] = y
    xb_ref[...] = y.astype(BF16)


def residual_layer_norm(x, mix, g, b, tm=256):
    s, d = x.shape
    tm = min(tm, s)
    row = pl.BlockSpec((tm, d), lambda i: (i, 0))
    vec = pl.BlockSpec((1, d), lambda i: (0, 0))
    return pl.pallas_call(
        _ln_kernel,
        grid=(s // tm,),
        in_specs=[row, row, vec, vec],
        out_specs=[row, row],
        out_shape=[jax.ShapeDtypeStruct((s, d), F32), jax.ShapeDtypeStruct((s, d), BF16)],
        compiler_params=_params("parallel"),
        name="residual_layer_norm",
    )(x, mix, g.reshape(1, d), b.reshape(1, d))


def _hgrn_kernel(lbl_ref, q_ref, f_ref, i_ref, g_ref, gn_ref, tri_ref, o_ref,
                 st_ref, qs_s, k_s, b_s, o_s, u_s, stb_s, *, layer, ts):
    dk = MIX_HEAD_DIM
    c = HGRN_SUB

    @pl.when(pl.program_id(1) == 0)
    def _():
        st_ref[...] = jnp.zeros_like(st_ref)

    lg = lbl_ref[...]
    e = jnp.exp(lg - jnp.max(lg, axis=0, keepdims=True))
    sm = e / jnp.sum(e, axis=0, keepdims=True)
    cum = sm[0:1]
    for l in range(1, layer + 1):
        cum = cum + sm[l:l + 1]
    lb = cum - sm[0:1]

    fz = f_ref[...]
    ls_pos, ls_neg = _log_sigmoid_parts(fz)
    la = jnp.log(jnp.maximum(lb, LB_TINY))
    lc = jnp.log1p(-lb) + ls_pos
    log_f = jnp.maximum(la, lc) + jnp.log(1.0 + jnp.exp(_neg_abs(la - lc)))
    q = q_ref[...]
    qs_s[...] = q * jax.nn.sigmoid(q)
    k_s[...] = (1.0 - lb) * jnp.exp(ls_neg)

    tri = tri_ref[...]
    tb = tri.shape[0]
    hi, lo = _split_bf16(log_f)
    for r0 in range(0, ts, tb):
        b_s[r0:r0 + tb, :] = _dot(tri, hi[r0:r0 + tb]) + _dot(tri, lo[r0:r0 + tb])

    ones = jnp.ones((dk, dk), BF16)
    t_idx = lax.broadcasted_iota(jnp.int32, (c, dk), 0)

    n_sub = ts // c

    def sub_rows(j):
        return pl.ds(pl.multiple_of(j * c, c), c)

    @functools.partial(lax.fori_loop, 0, n_sub, init_val=None, unroll=True)
    def _(j, _):
        rows = sub_rows(j)
        b = b_s[rows, :]
        kh = k_s[rows, :] * jnp.exp(b[c - 1:c, :] - b)
        u_s[j] = _dot_tn(i_ref[rows, :].astype(BF16), kh.astype(BF16))

    def advance(j, st):
        stb_s[j] = st.astype(BF16)
        return st * jnp.exp(b_s[pl.ds(j * c + c - 1, 1), :]) + u_s[j]

    st_ref[...] = lax.fori_loop(0, n_sub, advance, st_ref[...], unroll=8)

    @functools.partial(lax.fori_loop, 0, n_sub, init_val=None, unroll=16)
    def _(j, _):
        rows = sub_rows(j)
        qs, kk, b, vv = qs_s[rows, :], k_s[rows, :], b_s[rows, :], i_ref[rows, :]
        o_inter = _dot_nt((qs * jnp.exp(b)).astype(BF16), stb_s[j])
        h = c // 2
        prods = []
        for s in range(c):
            lo = 0 if s < h else h
            p = qs[lo:] * kk[s:s + 1, :] * jnp.exp(b[lo:] - b[s:s + 1, :])
            prods.append(jnp.where(t_idx[lo:] >= s, p, 0.0))
        scores = _dot(jnp.concatenate(prods, axis=0).astype(BF16), ones)
        o_top = scores[0:h] * vv[0:1, :]
        o_bot = scores[h:c] * vv[0:1, :]
        for s in range(1, h):
            o_top = o_top + scores[s * c:s * c + h] * vv[s:s + 1, :]
            o_bot = o_bot + scores[s * c + h:(s + 1) * c] * vv[s:s + 1, :]
        for s in range(h, c):
            r0 = h * c + (s - h) * h
            o_bot = o_bot + scores[r0:r0 + h] * vv[s:s + 1, :]
        o_s[rows, :] = o_inter + jnp.concatenate([o_top, o_bot], axis=0)

    of = o_s[...]
    of = of * lax.rsqrt(jnp.mean(of * of, axis=-1, keepdims=True) + RMS_EPS)
    g = g_ref[...]
    o_ref[...] = (of * gn_ref[...] * (g * jax.nn.sigmoid(g))).astype(o_ref.dtype)


def hgrn2_mix(y, lb_logits, onorm_g, layer, n_heads, ts=1024):
    s = y.shape[0]
    dk = MIX_HEAD_DIM
    ts = min(ts, s)
    n_layers = lb_logits.shape[0]

    def col(group):
        return pl.BlockSpec((ts, dk), lambda h, i, group=group: (i, group * n_heads + h))

    tb = min(MIX_HEAD_DIM, ts)
    r = lax.broadcasted_iota(jnp.int32, (tb, tb), 0)
    cc = lax.broadcasted_iota(jnp.int32, (tb, tb), 1)
    tri = ((r // HGRN_SUB == cc // HGRN_SUB) & (cc <= r)).astype(BF16)

    return pl.pallas_call(
        functools.partial(_hgrn_kernel, layer=layer, ts=ts),
        grid=(n_heads, s // ts),
        in_specs=[pl.BlockSpec((n_layers, dk), lambda h, i: (0, h)),
                  col(0), col(1), col(2), col(3),
                  pl.BlockSpec((1, dk), lambda h, i: (0, h)),
                  pl.BlockSpec((tb, tb), lambda h, i: (0, 0))],
        out_specs=pl.BlockSpec((ts, dk), lambda h, i: (i, h)),
        out_shape=jax.ShapeDtypeStruct((s, n_heads * dk), BF16),
        scratch_shapes=([pltpu.VMEM((dk, dk), F32)] + [pltpu.VMEM((ts, dk), F32)] * 4
                        + [pltpu.VMEM((ts // HGRN_SUB, dk, dk), F32), pltpu.VMEM((ts // HGRN_SUB, dk, dk), BF16)]),
        compiler_params=_params("parallel", "arbitrary"),
        name="hgrn2_mix",
    )(lb_logits, y, y, y, y, onorm_g.reshape(1, -1), tri)


def _sb_kernel(q_ref, k_ref, v_ref, o_ref, acc_ref, rest_ref, *, t, g):
    d = MIX_HEAD_DIM
    scale2 = d ** -0.5 * LOG2_E
    qi = pl.program_id(1)
    jj = lax.broadcasted_iota(jnp.int32, (t, t + d), 0)
    ss = lax.broadcasted_iota(jnp.int32, (t, t + d), 1)
    later = jnp.where((jj > ss) | (ss >= t), 1.0, 0.0).astype(BF16)
    causal = lax.broadcasted_iota(jnp.int32, (t, t), 1) < lax.broadcasted_iota(jnp.int32, (t, t), 0)

    def key_blocks(kb, on_diagonal):
        rows = pl.ds(pl.multiple_of(kb * t, t), t)
        heads = [slice(h * d, (h + 1) * d) for h in range(g)]
        z2 = [_dot_nt(q_ref[:, c], k_ref[rows, c]) * scale2 for c in heads]
        log_beta, log_rest = [], []
        for z in z2:
            lb = jnp.minimum(z, 0.0) - jnp.log2(1.0 + jnp.exp2(_neg_abs(z)))
            lr = lb - z
            log_beta.append(lb)
            log_rest.append(jnp.where(causal, lr, 0.0) if on_diagonal else lr)
        sums = [_dot(lr.astype(BF16), later) for lr in log_rest]
        for c, lb, sm in zip(heads, log_beta, sums):
            if on_diagonal:
                w = jnp.where(causal, jnp.exp2(lb + sm[:, :t]), 0.0)
                acc_ref[:, c] = _dot(w.astype(BF16), v_ref[rows, c])
                rest_ref[:, c] = sm[:, t:]
            else:
                rest = rest_ref[:, c]
                w = jnp.exp2(lb + sm[:, :t] + jnp.concatenate([rest] * (t // d), axis=1))
                acc_ref[:, c] += _dot(w.astype(BF16), v_ref[rows, c])
                rest_ref[:, c] = rest + sm[:, t:]

    def any_weight_left():
        return jnp.max(rest_ref[...]) > SB_UNDERFLOW_LOG2

    key_blocks(qi, True)

    def earlier_block(carry):
        i, _ = carry
        key_blocks(qi - 1 - i, False)
        return i + 1, any_weight_left()

    lax.while_loop(lambda carry: (carry[0] < qi) & carry[1], earlier_block, (jnp.int32(0), any_weight_left()))

    o_ref[...] = acc_ref[...].astype(o_ref.dtype)


def stick_breaking_mix(qy, kv, n_heads, t=256, g=8):
    s = qy.shape[0]
    d = MIX_HEAD_DIM
    t = min(t, s)
    g = min(g, n_heads)
    assert t % d == 0 and s % t == 0 and n_heads % g == 0
    return pl.pallas_call(
        functools.partial(_sb_kernel, t=t, g=g),
        grid=(n_heads // g, s // t),
        in_specs=[pl.BlockSpec((t, g * d), lambda h, i: (i, h)),
                  pl.BlockSpec((s, g * d), lambda h, i: (0, h), pipeline_mode=pl.Buffered(1)),
                  pl.BlockSpec((s, g * d), lambda h, i: (0, n_heads // g + h), pipeline_mode=pl.Buffered(1))],
        out_specs=pl.BlockSpec((t, g * d), lambda h, i: (i, h)),
        out_shape=jax.ShapeDtypeStruct((s, n_heads * d), BF16),
        scratch_shapes=[pltpu.VMEM((t, g * d), F32)] * 2,
        compiler_params=_params("parallel", "parallel"),
        name="stick_breaking_mix",
    )(qy, kv, kv)


def _mem_kernel(q_ref, k_ref, v_ref, o_ref):
    hd = q_ref.shape[-1]
    s = _dot_nt(q_ref[...].astype(BF16), k_ref[...]) * (hd ** -0.5)
    e = jnp.exp(s - jnp.max(s, axis=-1, keepdims=True))
    p = e / jnp.sum(e, axis=-1, keepdims=True)
    o_ref[...] = _dot(p.astype(BF16), v_ref[...]).astype(o_ref.dtype)


def memory_mix(qy, q_offset, mem_kv, tq=512):
    s = qy.shape[0]
    n_mem, w2 = mem_kv.shape
    hd = w2 // 2 // MEM_HEADS
    tq = min(tq, s)
    q0 = q_offset // hd
    return pl.pallas_call(
        _mem_kernel,
        grid=(s // tq, MEM_HEADS),
        in_specs=[pl.BlockSpec((tq, hd), lambda i, h: (i, q0 + h)),
                  pl.BlockSpec((n_mem, hd), lambda i, h: (0, h)),
                  pl.BlockSpec((n_mem, hd), lambda i, h: (0, MEM_HEADS + h))],
        out_specs=pl.BlockSpec((tq, hd), lambda i, h: (i, h)),
        out_shape=jax.ShapeDtypeStruct((s, MEM_HEADS * hd), BF16),
        compiler_params=_params("parallel", "parallel"),
        name="memory_mix",
    )(qy, mem_kv, mem_kv)


def _ffn_in_kernel(x_ref, wg_ref, wu_ref, cw_ref, cb_ref, wo_ref, o_ref, wob_ref, tail_ref, wgb_ref, wub_ref):
    tm = x_ref.shape[0]
    wob_ref[...] = wo_ref[...].astype(BF16)

    @pl.when(pl.program_id(1) == 0)
    def _():
        tail_ref[...] = jnp.zeros_like(tail_ref)
        wgb_ref[...] = wg_ref[...].astype(BF16)
        wub_ref[...] = wu_ref[...].astype(BF16)

    x = x_ref[...]
    gate = _dot(x, wgb_ref[...])
    up = _dot(x, wub_ref[...])
    tail = tail_ref[...]
    row = lax.broadcasted_iota(jnp.int32, gate.shape, 0)
    g1 = jnp.where(row == 0, tail[SUBLANES - 1:SUBLANES, :], pltpu.roll(gate, 1, 0))
    g2 = jnp.where(row == 0, tail[SUBLANES - 2:SUBLANES - 1, :],
                   jnp.where(row == 1, tail[SUBLANES - 1:SUBLANES, :], pltpu.roll(gate, 2, 0)))
    cw = cw_ref[...]
    conv = cb_ref[...] + cw[0:1, :] * g2
    conv = conv + cw[1:2, :] * g1
    conv = conv + cw[2:3, :] * gate
    tail_ref[...] = gate[tm - SUBLANES:tm, :]
    o_ref[...] = (conv * jax.nn.sigmoid(conv) * up).astype(o_ref.dtype)


def conv_glu_in(x, w_in, conv_w, conv_b, w_out, layer, tm=1024, tn=256):
    s, d = x.shape
    d_ff = w_in.shape[2] // 2
    tm = min(tm, s)
    tn = min(tn, d_ff)
    assert s % tm == 0 and d_ff % tn == 0 and CONV_WIDTH == 3
    nt, nm = d_ff // tn, s // tm
    cast_rows = d_ff // (nt * nm)
    assert cast_rows * nt * nm == d_ff and cast_rows % (2 * SUBLANES) == 0, (d_ff, nt, nm)
    return pl.pallas_call(
        _ffn_in_kernel,
        grid=(nt, nm),
        in_specs=[pl.BlockSpec((tm, d), lambda j, i: (i, 0)),
                  pl.BlockSpec((None, d, tn), lambda j, i: (layer, 0, j)),
                  pl.BlockSpec((None, d, tn), lambda j, i: (layer, 0, nt + j)),
                  pl.BlockSpec((None, CONV_WIDTH, tn), lambda j, i: (layer, 0, j)),
                  pl.BlockSpec((None, 1, tn), lambda j, i: (layer, 0, j)),
                  pl.BlockSpec((None, cast_rows, d), lambda j, i: (layer, j * nm + i, 0))],
        out_specs=[pl.BlockSpec((tm, tn), lambda j, i: (i, j)),
                   pl.BlockSpec((cast_rows, d), lambda j, i: (j * nm + i, 0))],
        out_shape=[jax.ShapeDtypeStruct((s, d_ff), BF16), jax.ShapeDtypeStruct((d_ff, d), BF16)],
        scratch_shapes=[pltpu.VMEM((SUBLANES, tn), F32), pltpu.VMEM((d, tn), BF16), pltpu.VMEM((d, tn), BF16)],
        compiler_params=_params("parallel", "arbitrary"),
        name="conv_glu_in",
    )(x, w_in, w_in, conv_w, conv_b.reshape(conv_b.shape[0], 1, d_ff), w_out)


def kernel(x, mem, a_w_in, hgrn_lb_logits, a_onorm_g, b_w_in, w_kv_shared, w_mem_kv, w_o,
           ffn_w_in, ffn_conv_w, ffn_conv_b, ffn_w_out, ln_g, ln_b):
    batch, seq, d_model = x.shape
    assert batch == 1
    mix_width = a_onorm_g.shape[1]
    n_heads = mix_width // MIX_HEAD_DIM
    bf = lambda t: t.astype(BF16)

    xf = x[0]
    xb = bf(xf)
    mem_b = bf(mem[0])
    kv = None
    for layer in range(DEPTH):
        mem_kv = matmul_f32w([mem_b], w_mem_kv, layer, BF16, 256, 512)
        if layer < N_A_LAYERS:
            y = matmul_f32w([xb], a_w_in, layer, F32, 1024, 512)
            o = hgrn2_mix(y, hgrn_lb_logits, a_onorm_g[layer], layer, n_heads)
            m = memory_mix(y, 4 * mix_width, mem_kv)
        else:
            if layer == N_A_LAYERS:
                kv = matmul_f32w([xb], w_kv_shared[None], 0, BF16, 1024, 512)
            y = matmul_f32w([xb], b_w_in, layer - N_A_LAYERS, BF16, 1024, 512)
            o = stick_breaking_mix(y, kv, n_heads)
            m = memory_mix(y, mix_width, mem_kv)
        mix = matmul_f32w([o, m], w_o, layer, BF16, 1024, 512)
        xf, xb = residual_layer_norm(xf, mix, ln_g[layer, 0], ln_b[layer, 0])
        act, w_out_b = conv_glu_in(xb, ffn_w_in, ffn_conv_w, ffn_conv_b, ffn_w_out, layer)
        ffn = matmul(act, w_out_b[None], 0, BF16, 512, 512)
        xf, xb = residual_layer_norm(xf, ffn, ln_g[layer, 1], ln_b[layer, 1])
    return xf[None]
```
